```python
import jax, jax.numpy as jnp
from jax import lax
import numpy as np

D_MODEL = 1024
BATCH = 8
SEQ = 2048
DEPTH = 2

CONV_DIM = D_MODEL
CONV_KERNEL = 31
POOL_DIM = D_MODEL
POOL_WINDOWS = (2, 4, 8, 16)
N_POOL_GROUPS = len(POOL_WINDOWS)
POOL_GROUP_DIM = POOL_DIM // N_POOL_GROUPS
POOL_OUT_GROUP_DIM = D_MODEL // N_POOL_GROUPS
D_FF = 2816
FFN_KERNEL = 3
IN_COLS = 2 * CONV_DIM + POOL_DIM + 2 * D_MODEL
EPS = 1e-6

kernel_name = "hybrid_gated_conformer_pool_block"


def rmsnorm(x, g):
    xf = x.astype(jnp.float32)
    y = xf * lax.rsqrt(jnp.mean(xf * xf, axis=-1, keepdims=True) + EPS)
    return (y * g.astype(jnp.float32)).astype(x.dtype)


def layernorm(x, g, b):
    xf = x.astype(jnp.float32)
    mu = jnp.mean(xf, axis=-1, keepdims=True)
    var = jnp.mean(jnp.square(xf - mu), axis=-1, keepdims=True)
    y = (xf - mu) * lax.rsqrt(var + EPS)
    return (y * g.astype(jnp.float32) + b.astype(jnp.float32)).astype(x.dtype)


def causal_dwconv(x, w, b):
    k, c = w.shape
    xp = jnp.pad(x, ((0, 0), (k - 1, 0), (0, 0)))
    y = lax.conv_general_dilated(
        xp, w[:, None, :].astype(x.dtype), window_strides=(1,), padding="VALID",
        dimension_numbers=("NWC", "WIO", "NWC"), feature_group_count=c)
    return y + b


def conformer_conv_branch(val, gate, dw_w, dw_b, ln_g, ln_b, w_out):
    u = val * jax.nn.sigmoid(gate)
    u = causal_dwconv(u, dw_w, dw_b)
    u = layernorm(u, ln_g, ln_b)
    u = jax.nn.silu(u)
    return jnp.einsum("bsc,cd->bsd", u, w_out)


def pool_branch(u, w_groups, scale):
    b, s, _ = u.shape
    uf = u.astype(jnp.float32).reshape(b, s, N_POOL_GROUPS, POOL_GROUP_DIM)
    cs = jnp.cumsum(uf, axis=1)
    pos1 = jnp.arange(1, s + 1, dtype=jnp.float32)[None, :, None]
    outs = []
    for g, w in enumerate(POOL_WINDOWS):
        c = cs[:, :, g]
        lag = jnp.pad(c[:, : s - w], ((0, 0), (w, 0), (0, 0)))
        cnt = jnp.minimum(pos1, float(w))
        outs.append((c - lag) / cnt - uf[:, :, g])
    pooled = jnp.stack(outs, axis=2).astype(u.dtype)
    y = jnp.einsum("bsgc,gcd->bsgd", pooled, w_groups).reshape(b, s, D_MODEL)
    return y * scale


def gdfn(h, w_up, dw_w, dw_b, w_down):
    u = jnp.einsum("bsd,df->bsf", h, w_up)
    u = causal_dwconv(u, dw_w, dw_b)
    a, g = jnp.split(u, 2, axis=-1)
    return jnp.einsum("bsf,fd->bsd", jax.nn.gelu(a) * g, w_down)


def setup_inputs(seed: int = 0) -> dict:
    key = jax.random.key(seed)
    ks = jax.random.split(key, 20)
    f32 = jnp.float32
    nrm = lambda k, shape: jax.random.normal(k, shape, f32)
    L, D = DEPTH, D_MODEL
    return {
        "x": nrm(ks[0], (BATCH, SEQ, D)),
        "w_in": nrm(ks[1], (L, D, IN_COLS)) * D ** -0.5,
        "b_in": nrm(ks[2], (L, IN_COLS)) * 0.02,
        "conv_dw_w": nrm(ks[3], (L, CONV_KERNEL, CONV_DIM)) * CONV_KERNEL ** -0.5,
        "conv_dw_b": nrm(ks[4], (L, CONV_DIM)) * 0.02,
        "conv_ln_g": 1.0 + 0.05 * nrm(ks[5], (L, CONV_DIM)),
        "conv_ln_b": 0.02 * nrm(ks[6], (L, CONV_DIM)),
        "conv_w_out": nrm(ks[7], (L, CONV_DIM, D)) * CONV_DIM ** -0.5,
        "pool_w": nrm(ks[8], (L, N_POOL_GROUPS, POOL_GROUP_DIM, POOL_OUT_GROUP_DIM)) * POOL_GROUP_DIM ** -0.5,
        "pool_scale": 1.0 + 0.05 * nrm(ks[9], (L, D)),
        "mix_w_out": nrm(ks[10], (L, D, D)) * D ** -0.5,
        "norm_mix": 1.0 + 0.05 * nrm(ks[11], (L, D)),
        "ffn_w_up": nrm(ks[12], (L, D, 2 * D_FF)) * D ** -0.5,
        "ffn_dw_w": nrm(ks[13], (L, FFN_KERNEL, 2 * D_FF)) * FFN_KERNEL ** -0.5,
        "ffn_dw_b": nrm(ks[14], (L, 2 * D_FF)) * 0.02,
        "ffn_w_down": nrm(ks[15], (L, D_FF, D)) * D_FF ** -0.5,
        "norm_ffn": 1.0 + 0.05 * nrm(ks[16], (L, D)),
        "norm_final": 1.0 + 0.05 * nrm(ks[17], (D,)),
    }


def reference(x, w_in, b_in, conv_dw_w, conv_dw_b, conv_ln_g, conv_ln_b, conv_w_out,
              pool_w, pool_scale, mix_w_out, norm_mix, ffn_w_up, ffn_dw_w, ffn_dw_b,
              ffn_w_down, norm_ffn, norm_final):
    c0 = CONV_DIM
    c1 = 2 * CONV_DIM
    c2 = c1 + POOL_DIM
    c3 = c2 + D_MODEL
    for l in range(DEPTH):
        h = rmsnorm(x, norm_mix[l])
        proj = jnp.einsum("bsd,dc->bsc", h, w_in[l]) + b_in[l]
        conv_val, conv_gate = proj[..., :c0], proj[..., c0:c1]
        pool_in = proj[..., c1:c2]
        gate_a = jax.nn.sigmoid(proj[..., c2:c3])
        gate_b = jax.nn.sigmoid(proj[..., c3:])
        ya = conformer_conv_branch(conv_val, conv_gate, conv_dw_w[l], conv_dw_b[l],
                                   conv_ln_g[l], conv_ln_b[l], conv_w_out[l])
        yb = pool_branch(pool_in, pool_w[l], pool_scale[l])
        merged = gate_a * ya + gate_b * yb
        x = x + jnp.einsum("bsd,de->bse", merged, mix_w_out[l])
        h = rmsnorm(x, norm_ffn[l])
        x = x + gdfn(h, ffn_w_up[l], ffn_dw_w[l], ffn_dw_b[l], ffn_w_down[l])
    return rmsnorm(x, norm_final)
```

```python
import functools
import math

import jax
import jax.numpy as jnp
from jax import lax
from jax.experimental import pallas as pl
from jax.experimental.pallas import tpu as pltpu

LANES = 128
SUBLANES = 8
VMEM_LIMIT_BYTES = 56 * 1024 * 1024

EPS = 1e-6
CONV_KERNEL = 31
POOL_WINDOWS = (2, 4, 8, 16)
FFN_KERNEL = 3

CONV_HALO = 32
POOL_HALO = 16
FFN_HALO = 8
ROW_BLOCK = 32
TIME_TILE = 256


def _rmsnorm(x, g):
    ms = jnp.mean(x * x, axis=-1, keepdims=True)
    return x * lax.rsqrt(ms + EPS) * g


def _sigmoid(x):
    return 1.0 / (1.0 + jnp.exp(-x))


def _gelu_tanh(x):
    c = math.sqrt(2.0 / math.pi)
    return 0.5 * x * (1.0 + jnp.tanh(c * (x + 0.044715 * (x * x * x))))


def _dot(a, b):
    return jnp.dot(a, b, preferred_element_type=jnp.float32)


def _mixer_kernel(x_ref, nrm_ref, win_ref, bin_ref, dww_ref, dwb_ref, lng_ref, lnb_ref,
                  cwo_ref, pw_ref, ps_ref, mwo_ref, o_ref,
                  u_ext, p_ext, conv_scr, pooled_scr, *, tm, d):
    j = pl.program_id(1)
    n_strips = d // LANES

    @pl.when(j == 0)
    def _():
        u_ext[:, 0:CONV_HALO, :] = jnp.zeros((n_strips, CONV_HALO, LANES), jnp.float32)
        p_ext[:, 0:POOL_HALO, :] = jnp.zeros((n_strips, POOL_HALO, LANES), jnp.float32)

    x = x_ref[0]
    h = _rmsnorm(x, nrm_ref[...]).astype(jnp.bfloat16)

    val = _dot(h, win_ref[:, 0:d]) + bin_ref[:, 0:d]
    gate = _dot(h, win_ref[:, d:2 * d]) + bin_ref[:, d:2 * d]
    u = val * _sigmoid(gate)
    pin = _dot(h, win_ref[:, 2 * d:3 * d]) + bin_ref[:, 2 * d:3 * d]
    for c in range(n_strips):
        u_ext[c, CONV_HALO:CONV_HALO + tm, :] = u[:, c * LANES:(c + 1) * LANES]
        p_ext[c, POOL_HALO:POOL_HALO + tm, :] = pin[:, c * LANES:(c + 1) * LANES]

    base = CONV_HALO - (CONV_KERNEL - 1)
    for c in range(n_strips):
        cs = slice(c * LANES, (c + 1) * LANES)
        ws = [jnp.broadcast_to(dww_ref[k:k + 1, cs], (SUBLANES, LANES)) for k in range(CONV_KERNEL)]
        bias = jnp.broadcast_to(dwb_ref[:, cs], (SUBLANES, LANES))

        def conv_body(i, carry, c=c, cs=cs, ws=ws, bias=bias):
            r = pl.multiple_of(i * ROW_BLOCK, ROW_BLOCK)
            for rr in range(0, ROW_BLOCK, SUBLANES):
                acc = ws[0] * u_ext[c, pl.ds(r + (rr + base), SUBLANES), :] + bias
                for k in range(1, CONV_KERNEL):
                    acc = acc + ws[k] * u_ext[c, pl.ds(r + (rr + base + k), SUBLANES), :]
                conv_scr[pl.ds(r + rr, SUBLANES), cs] = acc
            return carry

        lax.fori_loop(0, tm // ROW_BLOCK, conv_body, 0)

    strips_per_group = n_strips // len(POOL_WINDOWS)
    for c in range(n_strips):
        cs = slice(c * LANES, (c + 1) * LANES)
        w = POOL_WINDOWS[c // strips_per_group]

        def pool_body(i, carry, c=c, cs=cs, w=w):
            r = pl.multiple_of(i * ROW_BLOCK, ROW_BLOCK)
            for rr in range(0, ROW_BLOCK, SUBLANES):
                tok = p_ext[c, pl.ds(r + (rr + POOL_HALO), SUBLANES), :]
                acc = tok
                for s in range(1, w):
                    acc = acc + p_ext[c, pl.ds(r + (rr + POOL_HALO - s), SUBLANES), :]
                pos1 = (j * tm + r + rr + 1 + lax.broadcasted_iota(jnp.int32, (SUBLANES, LANES), 0)).astype(jnp.float32)
                cnt = jnp.minimum(pos1, float(w))
                pooled_scr[pl.ds(r + rr, SUBLANES), cs] = acc / cnt - tok
            return carry

        lax.fori_loop(0, tm // ROW_BLOCK, pool_body, 0)

    u_ext[:, 0:CONV_HALO, :] = u_ext[:, tm:tm + CONV_HALO, :]
    p_ext[:, 0:POOL_HALO, :] = p_ext[:, tm:tm + POOL_HALO, :]

    y = conv_scr[...]
    mu = jnp.mean(y, axis=-1, keepdims=True)
    yc = y - mu
    var = jnp.mean(yc * yc, axis=-1, keepdims=True)
    yn = yc * lax.rsqrt(var + EPS) * lng_ref[...] + lnb_ref[...]
    act = (yn * _sigmoid(yn)).astype(jnp.bfloat16)
    ya = _dot(act, cwo_ref[...])

    gd = d // len(POOL_WINDOWS)
    yb = jnp.concatenate(
        [_dot(pooled_scr[:, g * gd:(g + 1) * gd].astype(jnp.bfloat16), pw_ref[g])
         for g in range(len(POOL_WINDOWS))], axis=-1) * ps_ref[...]

    ga = _sigmoid(_dot(h, win_ref[:, 3 * d:4 * d]) + bin_ref[:, 3 * d:4 * d])
    gb = _sigmoid(_dot(h, win_ref[:, 4 * d:5 * d]) + bin_ref[:, 4 * d:5 * d])
    merged = (ga * ya + gb * yb).astype(jnp.bfloat16)
    o_ref[0] = x + _dot(merged, mwo_ref[...])


def _ffn_kernel(x_ref, nrm_ref, wup_ref, dww_ref, dwb_ref, wdn_ref, nf_ref, o_ref,
                u_ext, act_scr, *, tm, d, dff, final_norm):
    j = pl.program_id(1)
    n_strips = 2 * dff // LANES
    half = dff // LANES

    @pl.when(j == 0)
    def _():
        u_ext[:, 0:FFN_HALO, :] = jnp.zeros((n_strips, FFN_HALO, LANES), jnp.float32)

    x = x_ref[0]
    h = _rmsnorm(x, nrm_ref[...]).astype(jnp.bfloat16)
    u = _dot(h, wup_ref[...])
    for c in range(n_strips):
        u_ext[c, FFN_HALO:FFN_HALO + tm, :] = u[:, c * LANES:(c + 1) * LANES]

    base = FFN_HALO - (FFN_KERNEL - 1)

    def conv3(c, r):
        cs = slice(c * LANES, (c + 1) * LANES)
        acc = dww_ref[0:1, cs] * u_ext[c, pl.ds(r + base, SUBLANES), :] + dwb_ref[:, cs]
        for k in range(1, FFN_KERNEL):
            acc = acc + dww_ref[k:k + 1, cs] * u_ext[c, pl.ds(r + (base + k), SUBLANES), :]
        return acc

    for c in range(half):
        def body(i, carry, c=c):
            r = pl.multiple_of(i * ROW_BLOCK, ROW_BLOCK)
            rows = []
            for rr in range(0, ROW_BLOCK, SUBLANES):
                a = conv3(c, r + rr)
                g = conv3(half + c, r + rr)
                rows.append(_gelu_tanh(a) * g)
            act_scr[pl.ds(r, ROW_BLOCK), c * LANES:(c + 1) * LANES] = (
                jnp.concatenate(rows, axis=0).astype(jnp.bfloat16))
            return carry

        lax.fori_loop(0, tm // ROW_BLOCK, body, 0)

    u_ext[:, 0:FFN_HALO, :] = u_ext[:, tm:tm + FFN_HALO, :]

    y = x + _dot(act_scr[...], wdn_ref[...])
    if final_norm:
        y = _rmsnorm(y, nf_ref[...])
    o_ref[0] = y


def _resident(shape):
    zeros = (0,) * len(shape)
    return pl.BlockSpec(shape, lambda b, j: zeros, pipeline_mode=pl.Buffered(1))


def _mixer(x, nrm, win, b_in, dww, dwb, lng, lnb, cwo, pw, ps, mwo, *, tm):
    bsz, seq, d = x.shape
    n_strips = d // LANES
    tile = pl.BlockSpec((1, tm, d), lambda b, j: (b, j, 0))
    args = (nrm, win, b_in, dww, dwb, lng, lnb, cwo, pw, ps, mwo)
    return pl.pallas_call(
        functools.partial(_mixer_kernel, tm=tm, d=d),
        grid=(bsz, seq // tm),
        in_specs=[tile] + [_resident(a.shape) for a in args],
        out_specs=tile,
        out_shape=jax.ShapeDtypeStruct(x.shape, x.dtype),
        scratch_shapes=[
            pltpu.VMEM((n_strips, CONV_HALO + tm, LANES), jnp.float32),
            pltpu.VMEM((n_strips, POOL_HALO + tm, LANES), jnp.float32),
            pltpu.VMEM((tm, d), jnp.float32),
            pltpu.VMEM((tm, d), jnp.float32),
        ],
        compiler_params=pltpu.CompilerParams(
            dimension_semantics=("arbitrary", "arbitrary"),
            vmem_limit_bytes=VMEM_LIMIT_BYTES),
        name="mixer",
    )(x, *args)


def _ffn(x, nrm, wup, dww, dwb, wdn, nf, *, tm, final_norm):
    bsz, seq, d = x.shape
    dff = wdn.shape[0]
    tile = pl.BlockSpec((1, tm, d), lambda b, j: (b, j, 0))
    args = (nrm, wup, dww, dwb, wdn, nf)
    return pl.pallas_call(
        functools.partial(_ffn_kernel, tm=tm, d=d, dff=dff, final_norm=final_norm),
        grid=(bsz, seq // tm),
        in_specs=[tile] + [_resident(a.shape) for a in args],
        out_specs=tile,
        out_shape=jax.ShapeDtypeStruct(x.shape, x.dtype),
        scratch_shapes=[
            pltpu.VMEM((2 * dff // LANES, FFN_HALO + tm, LANES), jnp.float32),
            pltpu.VMEM((tm, dff), jnp.bfloat16),
        ],
        compiler_params=pltpu.CompilerParams(
            dimension_semantics=("arbitrary", "arbitrary"),
            vmem_limit_bytes=VMEM_LIMIT_BYTES),
        name="ffn",
    )(x, *args)


def kernel(x, w_in, b_in, conv_dw_w, conv_dw_b, conv_ln_g, conv_ln_b, conv_w_out, pool_w, pool_scale, mix_w_out, norm_mix, ffn_w_up, ffn_dw_w, ffn_dw_b, ffn_w_down, norm_ffn, norm_final):
    depth = w_in.shape[0]
    bf16 = jnp.bfloat16
    row = lambda v: v.reshape(1, -1)
    tm = TIME_TILE
    for l in range(depth):
        x = _mixer(x, row(norm_mix[l]), w_in[l].astype(bf16), row(b_in[l]),
                   conv_dw_w[l], row(conv_dw_b[l]), row(conv_ln_g[l]), row(conv_ln_b[l]),
                   conv_w_out[l].astype(bf16), pool_w[l].astype(bf16), row(pool_scale[l]),
                   mix_w_out[l].astype(bf16), tm=tm)
        x = _ffn(x, row(norm_ffn[l]), ffn_w_up[l].astype(bf16), ffn_dw_w[l], row(ffn_dw_b[l]),
                 ffn_w_down[l].astype(bf16), row(norm_final), tm=tm,
                 final_norm=(l == depth - 1))
    return x
```

```python
import functools
import math

import jax
import jax.numpy as jnp
from jax import lax
from jax.experimental import pallas as pl
from jax.experimental.pallas import tpu as pltpu

LANES = 128
SUBLANES = 8
MXU_COLS = 256
VMEM_LIMIT_BYTES = 56 * 1024 * 1024

EPS = 1e-6
CONV_KERNEL = 31
POOL_WINDOWS = (2, 4, 8, 16)
FFN_KERNEL = 3

CONV_HALO = 32
POOL_HALO = 16
FFN_HALO = 8
PACKED_ROWS = 16
CONV_GROUP_ROWS = 64
TIME_TILE = 256
STRIPS_PER_CHUNK = MXU_COLS // LANES


def _rmsnorm(x, g):
    ms = jnp.mean(x * x, axis=-1, keepdims=True)
    return x * lax.rsqrt(ms + EPS) * g


def _sigmoid(x):
    return 1.0 / (1.0 + jnp.exp(-x))


def _gelu_tanh(x):
    c = math.sqrt(2.0 / math.pi)
    return 0.5 * x * (1.0 + jnp.tanh(c * (x + 0.044715 * (x * x * x))))


def _dot(a, b):
    return jnp.dot(a, b, preferred_element_type=jnp.float32)


def _vreg_row(ref, k, cs):
    return jnp.broadcast_to(ref[k:k + 1, cs], (SUBLANES, LANES))


def _mixer_kernel(x_ref, nrm_ref, win_ref, bin_ref, dww_ref, dwb_ref, lng_ref, lnb_ref,
                  cwo_ref, pw_ref, ps_ref, mwo_ref, o_ref,
                  u_ext, p_ext, conv_scr, pooled_scr, *, tm, d):
    j = pl.program_id(1)
    n_strips = d // LANES
    assert d // MXU_COLS == len(POOL_WINDOWS)

    @pl.when(j == 0)
    def _():
        u_ext[:, 0:CONV_HALO, :] = jnp.zeros((n_strips, CONV_HALO, LANES), jnp.float32)
        p_ext[:, 0:POOL_HALO, :] = jnp.zeros((n_strips, POOL_HALO, LANES), jnp.float32)

    x = x_ref[0]
    h = _rmsnorm(x, nrm_ref[...]).astype(jnp.bfloat16)

    def proj(col0, cols):
        sl = slice(col0 + cols.start, col0 + cols.stop)
        return _dot(h, win_ref[:, sl]) + bin_ref[:, sl]

    conv_base = CONV_HALO - (CONV_KERNEL - 1)
    row_iota = lax.broadcasted_iota(jnp.int32, (SUBLANES, LANES), 0)
    def project_chunk(q):
        cols = slice(q * MXU_COLS, (q + 1) * MXU_COLS)
        u = proj(0, cols) * _sigmoid(proj(d, cols))
        pin = proj(2 * d, cols)
        for s in range(STRIPS_PER_CHUNK):
            c = q * STRIPS_PER_CHUNK + s
            ls = slice(s * LANES, (s + 1) * LANES)
            u_ext[c, CONV_HALO:CONV_HALO + tm, :] = u[:, ls]
            p_ext[c, POOL_HALO:POOL_HALO + tm, :] = pin[:, ls]

    def conv_pool_chunk(q):
        window = POOL_WINDOWS[q]
        for s in range(STRIPS_PER_CHUNK):
            c = q * STRIPS_PER_CHUNK + s
            cs = slice(c * LANES, (c + 1) * LANES)
            ws = [_vreg_row(dww_ref, k, cs) for k in range(CONV_KERNEL)]
            bias = _vreg_row(dwb_ref, 0, cs)
            for r0 in range(0, tm, CONV_GROUP_ROWS):
                blocks = range(r0, r0 + CONV_GROUP_ROWS, SUBLANES)
                accs = {r: bias for r in blocks}
                for lo in range(r0 + conv_base, r0 + conv_base + CONV_GROUP_ROWS + CONV_KERNEL - 1):
                    taps = [(r, lo - conv_base - r) for r in blocks
                            if 0 <= lo - conv_base - r < CONV_KERNEL]
                    if not taps:
                        continue
                    win = u_ext[c, lo:lo + SUBLANES, :]
                    for r, k in taps:
                        accs[r] = accs[r] + ws[k] * win
                for r in blocks:
                    conv_scr[r:r + SUBLANES, cs] = accs[r]
            for r in range(0, tm, SUBLANES):
                tok = p_ext[c, r + POOL_HALO:r + POOL_HALO + SUBLANES, :]
                acc = tok
                for back in range(1, window):
                    lo = r + POOL_HALO - back
                    acc = acc + p_ext[c, lo:lo + SUBLANES, :]
                pos1 = (j * tm + (r + 1) + row_iota).astype(jnp.float32)
                cnt = jnp.minimum(pos1, float(window))
                pooled_scr[r:r + SUBLANES, cs] = acc / cnt - tok

    n_chunks = len(POOL_WINDOWS)
    project_chunk(0)
    for q in range(n_chunks):
        if q + 1 < n_chunks:
            project_chunk(q + 1)
        else:
            ga = _sigmoid(_dot(h, win_ref[:, 3 * d:4 * d]) + bin_ref[:, 3 * d:4 * d])
        conv_pool_chunk(q)

    u_ext[:, 0:CONV_HALO, :] = u_ext[:, tm:tm + CONV_HALO, :]
    p_ext[:, 0:POOL_HALO, :] = p_ext[:, tm:tm + POOL_HALO, :]

    gb = _sigmoid(_dot(h, win_ref[:, 4 * d:5 * d]) + bin_ref[:, 4 * d:5 * d])

    y = conv_scr[...]
    mu = jnp.mean(y, axis=-1, keepdims=True)
    yc = y - mu
    var = jnp.mean(yc * yc, axis=-1, keepdims=True)
    yn = yc * lax.rsqrt(var + EPS) * lng_ref[...] + lnb_ref[...]
    act = (yn * _sigmoid(yn)).astype(jnp.bfloat16)

    yb = jnp.concatenate(
        [_dot(pooled_scr[:, g * MXU_COLS:(g + 1) * MXU_COLS].astype(jnp.bfloat16), pw_ref[g])
         for g in range(len(POOL_WINDOWS))], axis=-1) * ps_ref[...]

    ya = _dot(act, cwo_ref[...])
    merged = (ga * ya + gb * yb).astype(jnp.bfloat16)
    o_ref[0] = x + _dot(merged, mwo_ref[...])


def _ffn_kernel(x_ref, nrm_ref, wup_ref, dww_ref, dwb_ref, wdn_ref, nf_ref, o_ref,
                u_ext, act_scr, *, tm, d, dff, final_norm):
    j = pl.program_id(1)
    n_strips = 2 * dff // LANES
    half = dff // LANES

    @pl.when(j == 0)
    def _():
        u_ext[:, 0:FFN_HALO, :] = jnp.zeros((n_strips, FFN_HALO, LANES), jnp.float32)

    x = x_ref[0]
    h = _rmsnorm(x, nrm_ref[...]).astype(jnp.bfloat16)

    base = FFN_HALO - (FFN_KERNEL - 1)

    def conv3(c, r):
        cs = slice(c * LANES, (c + 1) * LANES)
        acc = _vreg_row(dww_ref, 0, cs) * u_ext[c, r + base:r + base + SUBLANES, :] + _vreg_row(dwb_ref, 0, cs)
        for k in range(1, FFN_KERNEL):
            lo = r + base + k
            acc = acc + _vreg_row(dww_ref, k, cs) * u_ext[c, lo:lo + SUBLANES, :]
        return acc

    n_chunks = dff // MXU_COLS

    def up_project(q):
        cols = slice(q * MXU_COLS, (q + 1) * MXU_COLS)
        a = _dot(h, wup_ref[:, cols])
        g = _dot(h, wup_ref[:, dff + q * MXU_COLS:dff + (q + 1) * MXU_COLS])
        for s in range(STRIPS_PER_CHUNK):
            c = q * STRIPS_PER_CHUNK + s
            ls = slice(s * LANES, (s + 1) * LANES)
            u_ext[c, FFN_HALO:FFN_HALO + tm, :] = a[:, ls]
            u_ext[half + c, FFN_HALO:FFN_HALO + tm, :] = g[:, ls]

    def activate(q):
        for s in range(STRIPS_PER_CHUNK):
            c = q * STRIPS_PER_CHUNK + s
            for r in range(0, tm, PACKED_ROWS):
                rows = [_gelu_tanh(conv3(c, r + rr)) * conv3(half + c, r + rr)
                        for rr in range(0, PACKED_ROWS, SUBLANES)]
                act_scr[r:r + PACKED_ROWS, c * LANES:(c + 1) * LANES] = (
                    jnp.concatenate(rows, axis=0).astype(jnp.bfloat16))
            for cc in (c, half + c):
                u_ext[cc, 0:FFN_HALO, :] = u_ext[cc, tm:tm + FFN_HALO, :]

    def down_project(q):
        cols = slice(q * MXU_COLS, (q + 1) * MXU_COLS)
        return _dot(act_scr[:, cols], wdn_ref[cols, :])

    y = x
    up_project(0)
    for q in range(n_chunks):
        if q + 1 < n_chunks:
            up_project(q + 1)
        activate(q)
        if q >= 1:
            y = y + down_project(q - 1)
    y = y + down_project(n_chunks - 1)

    if final_norm:
        y = _rmsnorm(y, nf_ref[...])
    o_ref[0] = y


def _resident(shape):
    zeros = (0,) * len(shape)
    return pl.BlockSpec(shape, lambda b, j: zeros, pipeline_mode=pl.Buffered(1))


def _mixer(x, nrm, win, b_in, dww, dwb, lng, lnb, cwo, pw, ps, mwo, *, tm):
    bsz, seq, d = x.shape
    n_strips = d // LANES
    tile = pl.BlockSpec((1, tm, d), lambda b, j: (b, j, 0))
    args = (nrm, win, b_in, dww, dwb, lng, lnb, cwo, pw, ps, mwo)
    return pl.pallas_call(
        functools.partial(_mixer_kernel, tm=tm, d=d),
        grid=(bsz, seq // tm),
        in_specs=[tile] + [_resident(a.shape) for a in args],
        out_specs=tile,
        out_shape=jax.ShapeDtypeStruct(x.shape, x.dtype),
        scratch_shapes=[
            pltpu.VMEM((n_strips, CONV_HALO + tm, LANES), jnp.float32),
            pltpu.VMEM((n_strips, POOL_HALO + tm, LANES), jnp.float32),
            pltpu.VMEM((tm, d), jnp.float32),
            pltpu.VMEM((tm, d), jnp.float32),
        ],
        compiler_params=pltpu.CompilerParams(
            dimension_semantics=("arbitrary", "arbitrary"),
            vmem_limit_bytes=VMEM_LIMIT_BYTES),
        name="mixer",
    )(x, *args)


def _ffn(x, nrm, wup, dww, dwb, wdn, nf, *, tm, final_norm):
    bsz, seq, d = x.shape
    dff = wdn.shape[0]
    tile = pl.BlockSpec((1, tm, d), lambda b, j: (b, j, 0))
    args = (nrm, wup, dww, dwb, wdn, nf)
    return pl.pallas_call(
        functools.partial(_ffn_kernel, tm=tm, d=d, dff=dff, final_norm=final_norm),
        grid=(bsz, seq // tm),
        in_specs=[tile] + [_resident(a.shape) for a in args],
        out_specs=tile,
        out_shape=jax.ShapeDtypeStruct(x.shape, x.dtype),
        scratch_shapes=[
            pltpu.VMEM((2 * dff // LANES, FFN_HALO + tm, LANES), jnp.float32),
            pltpu.VMEM((tm, dff), jnp.bfloat16),
        ],
        compiler_params=pltpu.CompilerParams(
            dimension_semantics=("arbitrary", "arbitrary"),
            vmem_limit_bytes=VMEM_LIMIT_BYTES),
        name="ffn",
    )(x, *args)


def kernel(x, w_in, b_in, conv_dw_w, conv_dw_b, conv_ln_g, conv_ln_b, conv_w_out, pool_w, pool_scale, mix_w_out, norm_mix, ffn_w_up, ffn_dw_w, ffn_dw_b, ffn_w_down, norm_ffn, norm_final):
    depth = w_in.shape[0]
    bf16 = jnp.bfloat16
    row = lambda v: v.reshape(1, -1)
    tm = TIME_TILE
    for l in range(depth):
        x = _mixer(x, row(norm_mix[l]), w_in[l].astype(bf16), row(b_in[l]),
                   conv_dw_w[l], row(conv_dw_b[l]), row(conv_ln_g[l]), row(conv_ln_b[l]),
                   conv_w_out[l].astype(bf16), pool_w[l].astype(bf16), row(pool_scale[l]),
                   mix_w_out[l].astype(bf16), tm=tm)
        x = _ffn(x, row(norm_ffn[l]), ffn_w_up[l].astype(bf16), ffn_dw_w[l], row(ffn_dw_b[l]),
                 ffn_w_down[l].astype(bf16), row(norm_final), tm=tm,
                 final_norm=(l == depth - 1))
    return x
```

```python
import functools
import math

import jax
import jax.numpy as jnp
from jax import lax
from jax.experimental import pallas as pl
from jax.experimental.pallas import tpu as pltpu

LANES = 128
SUBLANES = 8
MXU_COLS = 256
VMEM_LIMIT_BYTES = 56 * 1024 * 1024

EPS = 1e-6
CONV_KERNEL = 31
POOL_WINDOWS = (2, 4, 8, 16)
FFN_KERNEL = 3

CONV_HALO = 32
POOL_HALO = 16
FFN_HALO = 8
PACKED_ROWS = 16
CONV_GROUP_ROWS = 64
TIME_TILE = 256
STRIPS_PER_CHUNK = MXU_COLS // LANES

GELU_C = math.sqrt(2.0 / math.pi)


def _rmsnorm(x, g):
    ms = jnp.mean(x * x, axis=-1, keepdims=True)
    return x * lax.rsqrt(ms + EPS) * g


def _sigmoid(x):
    return 1.0 / (1.0 + jnp.exp(-x))


def _gelu_tanh(x):
    hx = 0.5 * x
    return hx + hx * jnp.tanh(x * (GELU_C + (0.044715 * GELU_C) * (x * x)))


def _dot(a, b):
    return jnp.dot(a, b, preferred_element_type=jnp.float32)


def _pack_bf16(w):
    wb = w.astype(jnp.bfloat16)
    lo = lax.bitcast_convert_type(wb[..., 0::2, :], jnp.uint16).astype(jnp.uint32)
    hi = lax.bitcast_convert_type(wb[..., 1::2, :], jnp.uint16).astype(jnp.uint32)
    return lax.bitcast_convert_type(lo | (hi << 16), jnp.int32)


def _unpack(packed):
    return pltpu.bitcast(packed, jnp.bfloat16)


def _vreg_row(ref, k, cs):
    return jnp.broadcast_to(ref[k:k + 1, cs], (SUBLANES, LANES))


def _mixer_kernel(x_ref, nrm_ref, win_ref, bin_ref, dww_ref, dwb_ref, lng_ref, lnb_ref,
                  cwo_ref, pw_ref, ps_ref, mwo_ref, o_ref,
                  u_ext, p_ext, h_scr, conv_scr, pooled_scr, ga_scr, gb_scr,
                  act_scr, pooled_bf, merged_scr, *, tm, d):
    j = pl.program_id(1)
    n_strips = d // LANES
    n_chunks = d // MXU_COLS
    assert n_chunks == len(POOL_WINDOWS)

    @pl.when(j == 0)
    def _():
        u_ext[:, 0:CONV_HALO, :] = jnp.zeros((n_strips, CONV_HALO, LANES), jnp.float32)
        p_ext[:, 0:POOL_HALO, :] = jnp.zeros((n_strips, POOL_HALO, LANES), jnp.float32)

    def proj(col0, cols):
        sl = slice(col0 + cols.start, col0 + cols.stop)
        return _dot(h_scr[...], _unpack(win_ref[:, sl])) + bin_ref[:, sl]

    conv_base = CONV_HALO - (CONV_KERNEL - 1)
    row_iota = lax.broadcasted_iota(jnp.int32, (SUBLANES, LANES), 0)

    def project_chunk(q):
        cols = slice(q * MXU_COLS, (q + 1) * MXU_COLS)
        u = proj(0, cols) * _sigmoid(proj(d, cols))
        pin = proj(2 * d, cols)
        for s in range(STRIPS_PER_CHUNK):
            c = q * STRIPS_PER_CHUNK + s
            ls = slice(s * LANES, (s + 1) * LANES)
            u_ext[c, CONV_HALO:CONV_HALO + tm, :] = u[:, ls]
            p_ext[c, POOL_HALO:POOL_HALO + tm, :] = pin[:, ls]

    def conv_pool_chunk(q, fillers=()):
        fillers = list(fillers)
        window = POOL_WINDOWS[q]
        for s in range(STRIPS_PER_CHUNK):
            c = q * STRIPS_PER_CHUNK + s
            cs = slice(c * LANES, (c + 1) * LANES)
            ws = [_vreg_row(dww_ref, k, cs) for k in range(CONV_KERNEL)]
            bias = _vreg_row(dwb_ref, 0, cs)
            for r0 in range(0, tm, CONV_GROUP_ROWS):
                blocks = range(r0, r0 + CONV_GROUP_ROWS, SUBLANES)
                accs = {r: bias for r in blocks}
                for lo in range(r0 + conv_base, r0 + conv_base + CONV_GROUP_ROWS + CONV_KERNEL - 1):
                    taps = [(r, lo - conv_base - r) for r in blocks
                            if 0 <= lo - conv_base - r < CONV_KERNEL]
                    win = u_ext[c, lo:lo + SUBLANES, :]
                    for r, k in taps:
                        accs[r] = accs[r] + ws[k] * win
                for r in blocks:
                    conv_scr[r:r + SUBLANES, cs] = accs[r]
                if fillers:
                    fillers.pop(0)()
            for r in range(0, tm, SUBLANES):
                tok = p_ext[c, r + POOL_HALO:r + POOL_HALO + SUBLANES, :]
                acc = tok
                for back in range(1, window):
                    lo = r + POOL_HALO - back
                    acc = acc + p_ext[c, lo:lo + SUBLANES, :]
                pos1 = (j * tm + (r + 1) + row_iota).astype(jnp.float32)
                cnt = jnp.minimum(pos1, float(window))
                pooled_scr[r:r + SUBLANES, cs] = acc / cnt - tok

    def gate_logits(col0, dst, n):
        cols = slice(n * MXU_COLS, (n + 1) * MXU_COLS)
        dst[:, cols] = _dot(h_scr[...], _unpack(win_ref[:, col0 + cols.start:col0 + cols.stop]))

    def gate(col0, src, cols):
        bias = bin_ref[:, col0 + cols.start:col0 + cols.stop]
        return _sigmoid(src[:, cols] + bias)

    def normalize():
        y = conv_scr[...]
        mu = jnp.mean(y, axis=-1, keepdims=True)
        yc = y - mu
        var = jnp.mean(yc * yc, axis=-1, keepdims=True)
        yn = yc * lax.rsqrt(var + EPS) * lng_ref[...] + lnb_ref[...]
        act_scr[...] = (yn * _sigmoid(yn)).astype(jnp.bfloat16)
        pooled_bf[...] = pooled_scr[...].astype(jnp.bfloat16)

    def merge_chunk(n):
        cols = slice(n * MXU_COLS, (n + 1) * MXU_COLS)
        ya = _dot(act_scr[...], _unpack(cwo_ref[:, cols]))
        yb = _dot(pooled_bf[:, cols], _unpack(pw_ref[n])) * ps_ref[:, cols]
        merged = gate(3 * d, ga_scr, cols) * ya + gate(4 * d, gb_scr, cols) * yb
        merged_scr[:, cols] = merged.astype(jnp.bfloat16)

    def out_chunk(n):
        cols = slice(n * MXU_COLS, (n + 1) * MXU_COLS)
        o_ref[0, :, cols] = x_ref[0, :, cols] + _dot(merged_scr[...], _unpack(mwo_ref[:, cols]))

    h_scr[...] = _rmsnorm(x_ref[0], nrm_ref[...]).astype(jnp.bfloat16)
    project_chunk(0)
    project_chunk(1)
    gate_logits(3 * d, ga_scr, 0)
    gate_logits(3 * d, ga_scr, 1)
    conv_pool_chunk(0)
    project_chunk(2)
    gate_logits(3 * d, ga_scr, 2)
    gate_logits(3 * d, ga_scr, 3)
    conv_pool_chunk(1)
    project_chunk(3)
    for n in range(n_chunks):
        gate_logits(4 * d, gb_scr, n)
    conv_pool_chunk(2)
    conv_pool_chunk(3)

    u_ext[:, 0:CONV_HALO, :] = u_ext[:, tm:tm + CONV_HALO, :]
    p_ext[:, 0:POOL_HALO, :] = p_ext[:, tm:tm + POOL_HALO, :]

    normalize()
    for n in range(n_chunks):
        merge_chunk(n)
    for n in range(n_chunks):
        out_chunk(n)


def _ffn_kernel(x_ref, nrm_ref, wup_ref, dww_ref, dwb_ref, wdn_ref, nf_ref, o_ref,
                u_ext, h_scr, act_scr, *, tm, d, dff, final_norm):
    j = pl.program_id(1)
    n_strips = 2 * dff // LANES
    half = dff // LANES

    @pl.when(j == 0)
    def _():
        u_ext[:, 0:FFN_HALO, :] = jnp.zeros((n_strips, FFN_HALO, LANES), jnp.float32)

    x = x_ref[0]
    h_scr[...] = _rmsnorm(x, nrm_ref[...]).astype(jnp.bfloat16)

    base = FFN_HALO - (FFN_KERNEL - 1)

    def conv3(c, r):
        cs = slice(c * LANES, (c + 1) * LANES)
        acc = _vreg_row(dww_ref, 0, cs) * u_ext[c, r + base:r + base + SUBLANES, :] + _vreg_row(dwb_ref, 0, cs)
        for k in range(1, FFN_KERNEL):
            lo = r + base + k
            acc = acc + _vreg_row(dww_ref, k, cs) * u_ext[c, lo:lo + SUBLANES, :]
        return acc

    n_chunks = dff // MXU_COLS

    def up_project(q):
        cols = slice(q * MXU_COLS, (q + 1) * MXU_COLS)
        a = _dot(h_scr[...], _unpack(wup_ref[:, cols]))
        g = _dot(h_scr[...], _unpack(wup_ref[:, dff + q * MXU_COLS:dff + (q + 1) * MXU_COLS]))
        for s in range(STRIPS_PER_CHUNK):
            c = q * STRIPS_PER_CHUNK + s
            ls = slice(s * LANES, (s + 1) * LANES)
            u_ext[c, FFN_HALO:FFN_HALO + tm, :] = a[:, ls]
            u_ext[half + c, FFN_HALO:FFN_HALO + tm, :] = g[:, ls]

    def activate(q):
        for s in range(STRIPS_PER_CHUNK):
            c = q * STRIPS_PER_CHUNK + s
            for r in range(0, tm, PACKED_ROWS):
                rows = [_gelu_tanh(conv3(c, r + rr)) * conv3(half + c, r + rr)
                        for rr in range(0, PACKED_ROWS, SUBLANES)]
                act_scr[r:r + PACKED_ROWS, c * LANES:(c + 1) * LANES] = (
                    jnp.concatenate(rows, axis=0).astype(jnp.bfloat16))
            for cc in (c, half + c):
                u_ext[cc, 0:FFN_HALO, :] = u_ext[cc, tm:tm + FFN_HALO, :]

    def down_project(q):
        cols = slice(q * MXU_COLS, (q + 1) * MXU_COLS)
        packed_rows = slice(cols.start // 2, cols.stop // 2)
        return _dot(act_scr[:, cols], _unpack(wdn_ref[packed_rows, :]))

    y = x
    up_project(0)
    for q in range(n_chunks):
        if q + 1 < n_chunks:
            up_project(q + 1)
        activate(q)
        if q >= 1:
            y = y + down_project(q - 1)
    y = y + down_project(n_chunks - 1)

    if final_norm:
        y = _rmsnorm(y, nf_ref[...])
    o_ref[0] = y


def _resident(shape):
    zeros = (0,) * len(shape)
    return pl.BlockSpec(shape, lambda b, j: zeros, pipeline_mode=pl.Buffered(1))


def _mixer(x, nrm, win, b_in, dww, dwb, lng, lnb, cwo, pw, ps, mwo, *, tm):
    bsz, seq, d = x.shape
    n_strips = d // LANES
    tile = pl.BlockSpec((1, tm, d), lambda b, j: (b, j, 0))
    args = (nrm, win, b_in, dww, dwb, lng, lnb, cwo, pw, ps, mwo)
    f32_tile = pltpu.VMEM((tm, d), jnp.float32)
    bf16_tile = pltpu.VMEM((tm, d), jnp.bfloat16)
    return pl.pallas_call(
        functools.partial(_mixer_kernel, tm=tm, d=d),
        grid=(bsz, seq // tm),
        in_specs=[tile] + [_resident(a.shape) for a in args],
        out_specs=tile,
        out_shape=jax.ShapeDtypeStruct(x.shape, x.dtype),
        scratch_shapes=[
            pltpu.VMEM((n_strips, CONV_HALO + tm, LANES), jnp.float32),
            pltpu.VMEM((n_strips, POOL_HALO + tm, LANES), jnp.float32),
            bf16_tile,
            f32_tile, f32_tile, f32_tile, f32_tile,
            bf16_tile, bf16_tile, bf16_tile,
        ],
        compiler_params=pltpu.CompilerParams(
            dimension_semantics=("arbitrary", "arbitrary"),
            vmem_limit_bytes=VMEM_LIMIT_BYTES),
        name="mixer",
    )(x, *args)


def _ffn(x, nrm, wup, dww, dwb, wdn, nf, *, tm, final_norm):
    bsz, seq, d = x.shape
    dff = 2 * wdn.shape[0]
    tile = pl.BlockSpec((1, tm, d), lambda b, j: (b, j, 0))
    args = (nrm, wup, dww, dwb, wdn, nf)
    return pl.pallas_call(
        functools.partial(_ffn_kernel, tm=tm, d=d, dff=dff, final_norm=final_norm),
        grid=(bsz, seq // tm),
        in_specs=[tile] + [_resident(a.shape) for a in args],
        out_specs=tile,
        out_shape=jax.ShapeDtypeStruct(x.shape, x.dtype),
        scratch_shapes=[
            pltpu.VMEM((2 * dff // LANES, FFN_HALO + tm, LANES), jnp.float32),
            pltpu.VMEM((tm, d), jnp.bfloat16),
            pltpu.VMEM((tm, dff), jnp.bfloat16),
        ],
        compiler_params=pltpu.CompilerParams(
            dimension_semantics=("arbitrary", "arbitrary"),
            vmem_limit_bytes=VMEM_LIMIT_BYTES),
        name="ffn",
    )(x, *args)


def kernel(x, w_in, b_in, conv_dw_w, conv_dw_b, conv_ln_g, conv_ln_b, conv_w_out, pool_w, pool_scale, mix_w_out, norm_mix, ffn_w_up, ffn_dw_w, ffn_dw_b, ffn_w_down, norm_ffn, norm_final):
    depth = w_in.shape[0]
    row = lambda v: v.reshape(1, -1)
    tm = TIME_TILE
    for l in range(depth):
        x = _mixer(x, row(norm_mix[l]), _pack_bf16(w_in[l]), row(b_in[l]),
                   conv_dw_w[l], row(conv_dw_b[l]), row(conv_ln_g[l]), row(conv_ln_b[l]),
                   _pack_bf16(conv_w_out[l]), _pack_bf16(pool_w[l]), row(pool_scale[l]),
                   _pack_bf16(mix_w_out[l]), tm=tm)
        x = _ffn(x, row(norm_ffn[l]), _pack_bf16(ffn_w_up[l]), ffn_dw_w[l], row(ffn_dw_b[l]),
                 _pack_bf16(ffn_w_down[l]), row(norm_final), tm=tm,
                 final_norm=(l == depth - 1))
    return x
```

```python
import functools
import math

import jax
import jax.numpy as jnp
from jax import lax
from jax.experimental import pallas as pl
from jax.experimental.pallas import tpu as pltpu

LANES = 128
SUBLANES = 8
MXU_COLS = 256
VMEM_LIMIT_BYTES = 56 * 1024 * 1024

EPS = 1e-6
CONV_KERNEL = 31
POOL_WINDOWS = (2, 4, 8, 16)
FFN_KERNEL = 3

CONV_HALO = 32
POOL_HALO = 16
FFN_HALO = 8
PACKED_ROWS = 16
CONV_GROUP_ROWS = 64
PACK_BLOCK_BYTES = 6 * 1024 * 1024
TIME_TILE = 256
STRIPS_PER_CHUNK = MXU_COLS // LANES

GELU_C = math.sqrt(2.0 / math.pi)


def _rmsnorm(x, g):
    ms = jnp.mean(x * x, axis=-1, keepdims=True)
    return x * lax.rsqrt(ms + EPS) * g


def _sigmoid(x):
    return 1.0 / (1.0 + jnp.exp(-x))


def _gelu_tanh(x):
    hx = 0.5 * x
    return hx + hx * jnp.tanh(x * (GELU_C + (0.044715 * GELU_C) * (x * x)))


def _dot(a, b):
    return jnp.dot(a, b, preferred_element_type=jnp.float32)


def _pack_kernel(w_ref, o_ref):
    o_ref[...] = pltpu.bitcast(w_ref[...].astype(jnp.bfloat16), jnp.int32)


def _pack_bf16(w):
    n_layers, k, n = w.shape
    bk = max(b for b in range(PACKED_ROWS, k + 1, PACKED_ROWS)
             if k % b == 0 and b * n * 4 <= PACK_BLOCK_BYTES)
    return pl.pallas_call(
        _pack_kernel,
        grid=(n_layers, k // bk),
        in_specs=[pl.BlockSpec((None, bk, n), lambda l, i: (l, i, 0))],
        out_specs=pl.BlockSpec((None, bk // 2, n), lambda l, i: (l, i, 0)),
        out_shape=jax.ShapeDtypeStruct((n_layers, k // 2, n), jnp.int32),
        compiler_params=pltpu.CompilerParams(
            dimension_semantics=("arbitrary", "arbitrary"),
            vmem_limit_bytes=VMEM_LIMIT_BYTES),
        name="pack_bf16",
    )(w)


def _unpack(packed):
    return pltpu.bitcast(packed, jnp.bfloat16)


def _vreg_row(ref, k, cs):
    return jnp.broadcast_to(ref[k:k + 1, cs], (SUBLANES, LANES))


def _mixer_kernel(x_ref, nrm_ref, win_ref, bin_ref, dww_ref, dwb_ref, lng_ref, lnb_ref,
                  cwo_ref, pw_ref, ps_ref, mwo_ref, o_ref,
                  u_ext, p_ext, h_scr, conv_scr, pooled_scr, ga_scr, gb_scr,
                  act_scr, pooled_bf, merged_scr, *, tm, d):
    j = pl.program_id(1)
    n_strips = d // LANES
    n_chunks = d // MXU_COLS
    assert n_chunks == len(POOL_WINDOWS)

    @pl.when(j == 0)
    def _():
        u_ext[:, 0:CONV_HALO, :] = jnp.zeros((n_strips, CONV_HALO, LANES), jnp.float32)
        p_ext[:, 0:POOL_HALO, :] = jnp.zeros((n_strips, POOL_HALO, LANES), jnp.float32)

    def proj(col0, cols):
        sl = slice(col0 + cols.start, col0 + cols.stop)
        return _dot(h_scr[...], _unpack(win_ref[:, sl])) + bin_ref[:, sl]

    conv_base = CONV_HALO - (CONV_KERNEL - 1)
    row_iota = lax.broadcasted_iota(jnp.int32, (SUBLANES, LANES), 0)

    def project_chunk(q):
        cols = slice(q * MXU_COLS, (q + 1) * MXU_COLS)
        u = proj(0, cols) * _sigmoid(proj(d, cols))
        pin = proj(2 * d, cols)
        for s in range(STRIPS_PER_CHUNK):
            c = q * STRIPS_PER_CHUNK + s
            ls = slice(s * LANES, (s + 1) * LANES)
            u_ext[c, CONV_HALO:CONV_HALO + tm, :] = u[:, ls]
            p_ext[c, POOL_HALO:POOL_HALO + tm, :] = pin[:, ls]

    def conv_pool_chunk(q, fillers=()):
        fillers = list(fillers)
        window = POOL_WINDOWS[q]
        for s in range(STRIPS_PER_CHUNK):
            c = q * STRIPS_PER_CHUNK + s
            cs = slice(c * LANES, (c + 1) * LANES)
            ws = [_vreg_row(dww_ref, k, cs) for k in range(CONV_KERNEL)]
            bias = _vreg_row(dwb_ref, 0, cs)
            for r0 in range(0, tm, CONV_GROUP_ROWS):
                blocks = range(r0, r0 + CONV_GROUP_ROWS, SUBLANES)
                accs = {r: bias for r in blocks}
                for lo in range(r0 + conv_base, r0 + conv_base + CONV_GROUP_ROWS + CONV_KERNEL - 1):
                    taps = [(r, lo - conv_base - r) for r in blocks
                            if 0 <= lo - conv_base - r < CONV_KERNEL]
                    win = u_ext[c, lo:lo + SUBLANES, :]
                    for r, k in taps:
                        accs[r] = accs[r] + ws[k] * win
                for r in blocks:
                    conv_scr[r:r + SUBLANES, cs] = accs[r]
                if fillers:
                    fillers.pop(0)()
            for r in range(0, tm, SUBLANES):
                tok = p_ext[c, r + POOL_HALO:r + POOL_HALO + SUBLANES, :]
                acc = tok
                for back in range(1, window):
                    lo = r + POOL_HALO - back
                    acc = acc + p_ext[c, lo:lo + SUBLANES, :]
                pos1 = (j * tm + (r + 1) + row_iota).astype(jnp.float32)
                cnt = jnp.minimum(pos1, float(window))
                pooled_scr[r:r + SUBLANES, cs] = acc / cnt - tok

    def gate_logits(col0, dst, n):
        cols = slice(n * MXU_COLS, (n + 1) * MXU_COLS)
        dst[:, cols] = _dot(h_scr[...], _unpack(win_ref[:, col0 + cols.start:col0 + cols.stop]))

    def gate(col0, src, cols):
        bias = bin_ref[:, col0 + cols.start:col0 + cols.stop]
        return _sigmoid(src[:, cols] + bias)

    def normalize():
        y = conv_scr[...]
        mu = jnp.mean(y, axis=-1, keepdims=True)
        yc = y - mu
        var = jnp.mean(yc * yc, axis=-1, keepdims=True)
        yn = yc * lax.rsqrt(var + EPS) * lng_ref[...] + lnb_ref[...]
        act_scr[...] = (yn * _sigmoid(yn)).astype(jnp.bfloat16)
        pooled_bf[...] = pooled_scr[...].astype(jnp.bfloat16)

    def merge_chunk(n):
        cols = slice(n * MXU_COLS, (n + 1) * MXU_COLS)
        ya = _dot(act_scr[...], _unpack(cwo_ref[:, cols]))
        group_rows = slice(cols.start // 2, cols.stop // 2)
        yb = _dot(pooled_bf[:, cols], _unpack(pw_ref[group_rows, :])) * ps_ref[:, cols]
        merged = gate(3 * d, ga_scr, cols) * ya + gate(4 * d, gb_scr, cols) * yb
        merged_scr[:, cols] = merged.astype(jnp.bfloat16)

    def out_chunk(n):
        cols = slice(n * MXU_COLS, (n + 1) * MXU_COLS)
        o_ref[0, :, cols] = x_ref[0, :, cols] + _dot(merged_scr[...], _unpack(mwo_ref[:, cols]))

    h_scr[...] = _rmsnorm(x_ref[0], nrm_ref[...]).astype(jnp.bfloat16)
    project_chunk(0)
    project_chunk(1)
    gate_logits(3 * d, ga_scr, 0)
    gate_logits(3 * d, ga_scr, 1)
    conv_pool_chunk(0)
    project_chunk(2)
    gate_logits(3 * d, ga_scr, 2)
    gate_logits(3 * d, ga_scr, 3)
    conv_pool_chunk(1)
    project_chunk(3)
    for n in range(n_chunks):
        gate_logits(4 * d, gb_scr, n)
    conv_pool_chunk(2)
    conv_pool_chunk(3)

    u_ext[:, 0:CONV_HALO, :] = u_ext[:, tm:tm + CONV_HALO, :]
    p_ext[:, 0:POOL_HALO, :] = p_ext[:, tm:tm + POOL_HALO, :]

    normalize()
    for n in range(n_chunks):
        merge_chunk(n)
    for n in range(n_chunks):
        out_chunk(n)


def _ffn_kernel(x_ref, nrm_ref, wup_ref, dww_ref, dwb_ref, wdn_ref, nf_ref, o_ref,
                u_ext, h_scr, act_scr, *, tm, d, dff, final_norm):
    j = pl.program_id(1)
    n_strips = 2 * dff // LANES
    half = dff // LANES

    @pl.when(j == 0)
    def _():
        u_ext[:, 0:FFN_HALO, :] = jnp.zeros((n_strips, FFN_HALO, LANES), jnp.float32)

    x = x_ref[0]
    h_scr[...] = _rmsnorm(x, nrm_ref[...]).astype(jnp.bfloat16)

    base = FFN_HALO - (FFN_KERNEL - 1)

    def conv3(c, r):
        cs = slice(c * LANES, (c + 1) * LANES)
        acc = _vreg_row(dww_ref, 0, cs) * u_ext[c, r + base:r + base + SUBLANES, :] + _vreg_row(dwb_ref, 0, cs)
        for k in range(1, FFN_KERNEL):
            lo = r + base + k
            acc = acc + _vreg_row(dww_ref, k, cs) * u_ext[c, lo:lo + SUBLANES, :]
        return acc

    n_chunks = dff // MXU_COLS

    def up_project(q):
        cols = slice(q * MXU_COLS, (q + 1) * MXU_COLS)
        a = _dot(h_scr[...], _unpack(wup_ref[:, cols]))
        g = _dot(h_scr[...], _unpack(wup_ref[:, dff + q * MXU_COLS:dff + (q + 1) * MXU_COLS]))
        for s in range(STRIPS_PER_CHUNK):
            c = q * STRIPS_PER_CHUNK + s
            ls = slice(s * LANES, (s + 1) * LANES)
            u_ext[c, FFN_HALO:FFN_HALO + tm, :] = a[:, ls]
            u_ext[half + c, FFN_HALO:FFN_HALO + tm, :] = g[:, ls]

    def activate(q):
        for s in range(STRIPS_PER_CHUNK):
            c = q * STRIPS_PER_CHUNK + s
            for r in range(0, tm, PACKED_ROWS):
                rows = [_gelu_tanh(conv3(c, r + rr)) * conv3(half + c, r + rr)
                        for rr in range(0, PACKED_ROWS, SUBLANES)]
                act_scr[r:r + PACKED_ROWS, c * LANES:(c + 1) * LANES] = (
                    jnp.concatenate(rows, axis=0).astype(jnp.bfloat16))
            for cc in (c, half + c):
                u_ext[cc, 0:FFN_HALO, :] = u_ext[cc, tm:tm + FFN_HALO, :]

    def down_project(q):
        cols = slice(q * MXU_COLS, (q + 1) * MXU_COLS)
        packed_rows = slice(cols.start // 2, cols.stop // 2)
        return _dot(act_scr[:, cols], _unpack(wdn_ref[packed_rows, :]))

    y = x
    up_project(0)
    for q in range(n_chunks):
        if q + 1 < n_chunks:
            up_project(q + 1)
        activate(q)
        if q >= 1:
            y = y + down_project(q - 1)
    y = y + down_project(n_chunks - 1)

    if final_norm:
        y = _rmsnorm(y, nf_ref[...])
    o_ref[0] = y


class _Layer:
    def __init__(self, stacked, layer):
        self.stacked, self.layer = stacked, layer


def _resident(arg):
    if isinstance(arg, _Layer):
        tail = arg.stacked.shape[1:]
        index = (arg.layer,) + (0,) * len(tail)
        return pl.BlockSpec((None,) + tail, lambda b, j: index, pipeline_mode=pl.Buffered(1))
    zeros = (0,) * arg.ndim
    return pl.BlockSpec(arg.shape, lambda b, j: zeros, pipeline_mode=pl.Buffered(1))


def _operand(arg):
    return arg.stacked if isinstance(arg, _Layer) else arg


def _mixer(x, nrm, win, b_in, dww, dwb, lng, lnb, cwo, pw, ps, mwo, *, tm):
    bsz, seq, d = x.shape
    n_strips = d // LANES
    tile = pl.BlockSpec((1, tm, d), lambda b, j: (b, j, 0))
    args = (nrm, win, b_in, dww, dwb, lng, lnb, cwo, pw, ps, mwo)
    f32_tile = pltpu.VMEM((tm, d), jnp.float32)
    bf16_tile = pltpu.VMEM((tm, d), jnp.bfloat16)
    return pl.pallas_call(
        functools.partial(_mixer_kernel, tm=tm, d=d),
        grid=(bsz, seq // tm),
        in_specs=[tile] + [_resident(a) for a in args],
        out_specs=tile,
        out_shape=jax.ShapeDtypeStruct(x.shape, x.dtype),
        scratch_shapes=[
            pltpu.VMEM((n_strips, CONV_HALO + tm, LANES), jnp.float32),
            pltpu.VMEM((n_strips, POOL_HALO + tm, LANES), jnp.float32),
            bf16_tile,
            f32_tile, f32_tile, f32_tile, f32_tile,
            bf16_tile, bf16_tile, bf16_tile,
        ],
        compiler_params=pltpu.CompilerParams(
            dimension_semantics=("arbitrary", "arbitrary"),
            vmem_limit_bytes=VMEM_LIMIT_BYTES),
        name="mixer",
    )(x, *[_operand(a) for a in args])


def _ffn(x, nrm, wup, dww, dwb, wdn, nf, *, tm, final_norm):
    bsz, seq, d = x.shape
    dff = 2 * wdn.stacked.shape[1]
    tile = pl.BlockSpec((1, tm, d), lambda b, j: (b, j, 0))
    args = (nrm, wup, dww, dwb, wdn, nf)
    return pl.pallas_call(
        functools.partial(_ffn_kernel, tm=tm, d=d, dff=dff, final_norm=final_norm),
        grid=(bsz, seq // tm),
        in_specs=[tile] + [_resident(a) for a in args],
        out_specs=tile,
        out_shape=jax.ShapeDtypeStruct(x.shape, x.dtype),
        scratch_shapes=[
            pltpu.VMEM((2 * dff // LANES, FFN_HALO + tm, LANES), jnp.float32),
            pltpu.VMEM((tm, d), jnp.bfloat16),
            pltpu.VMEM((tm, dff), jnp.bfloat16),
        ],
        compiler_params=pltpu.CompilerParams(
            dimension_semantics=("arbitrary", "arbitrary"),
            vmem_limit_bytes=VMEM_LIMIT_BYTES),
        name="ffn",
    )(x, *[_operand(a) for a in args])


def kernel(x, w_in, b_in, conv_dw_w, conv_dw_b, conv_ln_g, conv_ln_b, conv_w_out, pool_w, pool_scale, mix_w_out, norm_mix, ffn_w_up, ffn_dw_w, ffn_dw_b, ffn_w_down, norm_ffn, norm_final):
    depth = w_in.shape[0]
    row = lambda v: v.reshape(1, -1)
    tm = TIME_TILE
    win, cwo, mwo = _pack_bf16(w_in), _pack_bf16(conv_w_out), _pack_bf16(mix_w_out)
    pw = _pack_bf16(pool_w.reshape(depth, -1, pool_w.shape[-1]))
    wup, wdn = _pack_bf16(ffn_w_up), _pack_bf16(ffn_w_down)
    for l in range(depth):
        x = _mixer(x, row(norm_mix[l]), _Layer(win, l), row(b_in[l]),
                   conv_dw_w[l], row(conv_dw_b[l]), row(conv_ln_g[l]), row(conv_ln_b[l]),
                   _Layer(cwo, l), _Layer(pw, l), row(pool_scale[l]), _Layer(mwo, l), tm=tm)
        x = _ffn(x, row(norm_ffn[l]), _Layer(wup, l), ffn_dw_w[l], row(ffn_dw_b[l]),
                 _Layer(wdn, l), row(norm_final), tm=tm, final_norm=(l == depth - 1))
    return x
```

```python
import functools
import math

import jax
import jax.numpy as jnp
from jax import lax
from jax.experimental import pallas as pl
from jax.experimental.pallas import tpu as pltpu

LANES = 128
SUBLANES = 8
MXU_COLS = 256
VMEM_LIMIT_BYTES = 60 * 1024 * 1024

EPS = 1e-6
CONV_KERNEL = 31
POOL_WINDOWS = (2, 4, 8, 16)
FFN_KERNEL = 3

CONV_HALO = 32
POOL_HALO = 16
FFN_HALO = 8
PACKED_ROWS = 16
CONV_GROUP_ROWS = 64
PACK_BLOCK_BYTES = 6 * 1024 * 1024
TIME_TILE = 256
STRIPS_PER_CHUNK = MXU_COLS // LANES

GELU_C = math.sqrt(2.0 / math.pi)


def _rmsnorm(x, g):
    ms = jnp.mean(x * x, axis=-1, keepdims=True)
    return x * lax.rsqrt(ms + EPS) * g


def _sigmoid(x):
    return 1.0 / (1.0 + jnp.exp(-x))


def _gelu_tanh(x):
    hx = 0.5 * x
    return hx + hx * jnp.tanh(x * (GELU_C + (0.044715 * GELU_C) * (x * x)))


def _dot(a, b):
    return jnp.dot(a, b, preferred_element_type=jnp.float32)


def _pack_kernel(w_ref, o_ref):
    o_ref[...] = pltpu.bitcast(w_ref[...].astype(jnp.bfloat16), jnp.int32)


def _pack_bf16(w):
    n_layers, k, n = w.shape
    bk = max(b for b in range(PACKED_ROWS, k + 1, PACKED_ROWS)
             if k % b == 0 and b * n * 4 <= PACK_BLOCK_BYTES)
    return pl.pallas_call(
        _pack_kernel,
        grid=(n_layers, k // bk),
        in_specs=[pl.BlockSpec((None, bk, n), lambda l, i: (l, i, 0))],
        out_specs=pl.BlockSpec((None, bk // 2, n), lambda l, i: (l, i, 0)),
        out_shape=jax.ShapeDtypeStruct((n_layers, k // 2, n), jnp.int32),
        compiler_params=pltpu.CompilerParams(
            dimension_semantics=("arbitrary", "arbitrary"),
            vmem_limit_bytes=VMEM_LIMIT_BYTES),
        name="pack_bf16",
    )(w)


def _unpack(packed):
    return pltpu.bitcast(packed, jnp.bfloat16)


def _vreg_row(ref, k, cs):
    return jnp.broadcast_to(ref[k:k + 1, cs], (SUBLANES, LANES))


class _MixerStage:
    def __init__(self, x_tile, out_tile, j, weights, scratch, *, tm, d):
        (self.nrm_ref, self.win_ref, self.bin_ref, self.dww_ref, self.dwb_ref, self.lng_ref,
         self.lnb_ref, self.cwo_ref, self.pw_ref, self.ps_ref, self.mwo_ref) = weights
        self.n_chunks = d // MXU_COLS
        assert self.n_chunks == len(POOL_WINDOWS)
        self.u_ext, scratch = scratch[:self.n_chunks], scratch[self.n_chunks:]
        self.p_ext, scratch = scratch[:self.n_chunks], scratch[self.n_chunks:]
        (self.h_scr, self.conv_scr, self.pooled_scr, self.ga_scr,
         self.gb_scr, self.act_scr, self.pooled_bf, self.merged_scr) = scratch
        self.x_tile, self.out_tile, self.j, self.tm, self.d = x_tile, out_tile, j, tm, d

    def zero_halos(self):
        for q in range(self.n_chunks):
            self.u_ext[q][:, 0:CONV_HALO, :] = jnp.zeros((STRIPS_PER_CHUNK, CONV_HALO, LANES), jnp.float32)
            self.p_ext[q][:, 0:POOL_HALO, :] = jnp.zeros((STRIPS_PER_CHUNK, POOL_HALO, LANES), jnp.float32)

    def prologue(self):
        self.h_scr[...] = _rmsnorm(self.x_tile[...], self.nrm_ref[...]).astype(jnp.bfloat16)

    def _proj(self, col0, cols):
        sl = slice(col0 + cols.start, col0 + cols.stop)
        return _dot(self.h_scr[...], _unpack(self.win_ref[:, sl])) + self.bin_ref[:, sl]

    def project_conv_in(self, q):
        tm, d = self.tm, self.d
        cols = slice(q * MXU_COLS, (q + 1) * MXU_COLS)
        u = self._proj(0, cols) * _sigmoid(self._proj(d, cols))
        for s in range(STRIPS_PER_CHUNK):
            self.u_ext[q][s, CONV_HALO:CONV_HALO + tm, :] = u[:, s * LANES:(s + 1) * LANES]

    def project_pool_in(self, q):
        tm, d = self.tm, self.d
        pin = self._proj(2 * d, slice(q * MXU_COLS, (q + 1) * MXU_COLS))
        for s in range(STRIPS_PER_CHUNK):
            self.p_ext[q][s, POOL_HALO:POOL_HALO + tm, :] = pin[:, s * LANES:(s + 1) * LANES]

    def conv_piece(self, q, s, r0):
        conv_base = CONV_HALO - (CONV_KERNEL - 1)
        c = q * STRIPS_PER_CHUNK + s
        cs = slice(c * LANES, (c + 1) * LANES)
        ws = [_vreg_row(self.dww_ref, k, cs) for k in range(CONV_KERNEL)]
        blocks = range(r0, r0 + CONV_GROUP_ROWS, SUBLANES)
        accs = {r: _vreg_row(self.dwb_ref, 0, cs) for r in blocks}
        for lo in range(r0 + conv_base, r0 + conv_base + CONV_GROUP_ROWS + CONV_KERNEL - 1):
            taps = [(r, lo - conv_base - r) for r in blocks
                    if 0 <= lo - conv_base - r < CONV_KERNEL]
            win = self.u_ext[q][s, lo:lo + SUBLANES, :]
            for r, k in taps:
                accs[r] = accs[r] + ws[k] * win
        for r in blocks:
            self.conv_scr[r:r + SUBLANES, cs] = accs[r]

    def pool_piece(self, q, s):
        tm = self.tm
        row_iota = lax.broadcasted_iota(jnp.int32, (SUBLANES, LANES), 0)
        window = POOL_WINDOWS[q]
        c = q * STRIPS_PER_CHUNK + s
        cs = slice(c * LANES, (c + 1) * LANES)
        for r in range(0, tm, SUBLANES):
            tok = self.p_ext[q][s, r + POOL_HALO:r + POOL_HALO + SUBLANES, :]
            acc = tok
            for back in range(1, window):
                lo = r + POOL_HALO - back
                acc = acc + self.p_ext[q][s, lo:lo + SUBLANES, :]
            pos1 = (self.j * tm + (r + 1) + row_iota).astype(jnp.float32)
            cnt = jnp.minimum(pos1, float(window))
            self.pooled_scr[r:r + SUBLANES, cs] = acc / cnt - tok

    def gate_logits(self, which, n):
        col0, dst = ((3 * self.d, self.ga_scr), (4 * self.d, self.gb_scr))[which]
        cols = slice(n * MXU_COLS, (n + 1) * MXU_COLS)
        dst[:, cols] = _dot(self.h_scr[...],
                            _unpack(self.win_ref[:, col0 + cols.start:col0 + cols.stop]))

    def _gate(self, which, cols):
        col0, src = ((3 * self.d, self.ga_scr), (4 * self.d, self.gb_scr))[which]
        return _sigmoid(src[:, cols] + self.bin_ref[:, col0 + cols.start:col0 + cols.stop])

    def carry_halos(self):
        tm = self.tm
        for q in range(self.n_chunks):
            self.u_ext[q][:, 0:CONV_HALO, :] = self.u_ext[q][:, tm:tm + CONV_HALO, :]
            self.p_ext[q][:, 0:POOL_HALO, :] = self.p_ext[q][:, tm:tm + POOL_HALO, :]

    def normalize_piece(self, r0):
        rows = slice(r0, r0 + CONV_GROUP_ROWS)
        y = self.conv_scr[rows, :]
        mu = jnp.mean(y, axis=-1, keepdims=True)
        yc = y - mu
        var = jnp.mean(yc * yc, axis=-1, keepdims=True)
        yn = yc * lax.rsqrt(var + EPS) * self.lng_ref[...] + self.lnb_ref[...]
        self.act_scr[rows, :] = (yn * _sigmoid(yn)).astype(jnp.bfloat16)
        self.pooled_bf[rows, :] = self.pooled_scr[rows, :].astype(jnp.bfloat16)

    def merge_chunk(self, n):
        cols = slice(n * MXU_COLS, (n + 1) * MXU_COLS)
        ya = _dot(self.act_scr[...], _unpack(self.cwo_ref[:, cols]))
        group_rows = slice(cols.start // 2, cols.stop // 2)
        yb = _dot(self.pooled_bf[:, cols], _unpack(self.pw_ref[group_rows, :])) * self.ps_ref[:, cols]
        merged = self._gate(0, cols) * ya + self._gate(1, cols) * yb
        self.merged_scr[:, cols] = merged.astype(jnp.bfloat16)

    def out_chunk(self, n):
        cols = slice(n * MXU_COLS, (n + 1) * MXU_COLS)
        self.out_tile[:, cols] = self.x_tile[:, cols] + _dot(
            self.merged_scr[...], _unpack(self.mwo_ref[:, cols]))


class _FfnStage:
    def __init__(self, x_tile, out_tile, weights, scratch, *, tm, d, dff, final_norm):
        self.nrm_ref, self.wup_ref, self.dww_ref, self.dwb_ref, self.wdn_ref, self.nf_ref = weights
        self.n_chunks = dff // MXU_COLS
        self.u_ext, (self.h_scr, self.act_scr) = scratch[:self.n_chunks], scratch[self.n_chunks:]
        self.x_tile, self.out_tile = x_tile, out_tile
        self.tm, self.d, self.dff, self.final_norm = tm, d, dff, final_norm
        self.y = None

    def zero_halos(self):
        for q in range(self.n_chunks):
            self.u_ext[q][:, 0:FFN_HALO, :] = jnp.zeros((2 * STRIPS_PER_CHUNK, FFN_HALO, LANES), jnp.float32)

    def prologue(self):
        x = self.x_tile[...]
        self.h_scr[...] = _rmsnorm(x, self.nrm_ref[...]).astype(jnp.bfloat16)
        self.y = x

    def _conv3(self, q, local, r):
        base = FFN_HALO - (FFN_KERNEL - 1)
        is_gate, s = divmod(local, STRIPS_PER_CHUNK)
        col0 = is_gate * self.dff + (q * STRIPS_PER_CHUNK + s) * LANES
        cs = slice(col0, col0 + LANES)
        strip = self.u_ext[q]
        acc = (_vreg_row(self.dww_ref, 0, cs) * strip[local, r + base:r + base + SUBLANES, :]
               + _vreg_row(self.dwb_ref, 0, cs))
        for k in range(1, FFN_KERNEL):
            lo = r + base + k
            acc = acc + _vreg_row(self.dww_ref, k, cs) * strip[local, lo:lo + SUBLANES, :]
        return acc

    def up_project(self, q):
        tm, dff = self.tm, self.dff
        cols = slice(q * MXU_COLS, (q + 1) * MXU_COLS)
        a = _dot(self.h_scr[...], _unpack(self.wup_ref[:, cols]))
        g = _dot(self.h_scr[...], _unpack(self.wup_ref[:, dff + cols.start:dff + cols.stop]))
        for s in range(STRIPS_PER_CHUNK):
            ls = slice(s * LANES, (s + 1) * LANES)
            self.u_ext[q][s, FFN_HALO:FFN_HALO + tm, :] = a[:, ls]
            self.u_ext[q][STRIPS_PER_CHUNK + s, FFN_HALO:FFN_HALO + tm, :] = g[:, ls]

    def activate_piece(self, q, s):
        tm = self.tm
        c = q * STRIPS_PER_CHUNK + s
        for r in range(0, tm, PACKED_ROWS):
            rows = [_gelu_tanh(self._conv3(q, s, r + rr))
                    * self._conv3(q, STRIPS_PER_CHUNK + s, r + rr)
                    for rr in range(0, PACKED_ROWS, SUBLANES)]
            self.act_scr[r:r + PACKED_ROWS, c * LANES:(c + 1) * LANES] = (
                jnp.concatenate(rows, axis=0).astype(jnp.bfloat16))
        for local in (s, STRIPS_PER_CHUNK + s):
            self.u_ext[q][local, 0:FFN_HALO, :] = self.u_ext[q][local, tm:tm + FFN_HALO, :]

    def down_project(self, q):
        cols = slice(q * MXU_COLS, (q + 1) * MXU_COLS)
        packed_rows = slice(cols.start // 2, cols.stop // 2)
        self.y = self.y + _dot(self.act_scr[:, cols], _unpack(self.wdn_ref[packed_rows, :]))

    def epilogue(self):
        y = self.y
        if self.final_norm:
            y = _rmsnorm(y, self.nf_ref[...])
        self.out_tile[...] = y


class _Item:
    def __init__(self, name, mxu, emit, needs=(), valu=0):
        self.name, self.mxu, self.emit, self.needs, self.valu = name, mxu, emit, tuple(needs), valu


def _emit_interleaved(plan, pieces):
    total_mxu = sum(it.mxu for it in plan)
    total_valu = sum(it.valu for it in plan) + sum(it.valu for it in pieces)
    done, pending = set(), list(pieces)
    spent = {"mxu": 0, "valu": 0}

    def emit(item):
        assert all(n in done for n in item.needs), (item.name, item.needs)
        item.emit()
        done.add(item.name)
        spent["mxu"] += item.mxu
        spent["valu"] += item.valu

    def emit_piece(name):
        piece = next(p for p in pending if p.name == name)
        pending.remove(piece)
        for need in piece.needs:
            if need not in done:
                emit_piece(need)
        emit(piece)

    for item in plan:
        for name in item.needs:
            if name not in done:
                emit_piece(name)
        emit(item)
        while spent["valu"] * total_mxu < spent["mxu"] * total_valu:
            ready = [p for p in pending if all(n in done for n in p.needs)]
            if not ready:
                break
            emit_piece(ready[0].name)
    for piece in list(pending):
        emit_piece(piece.name)


def _layer_kernel(*refs, tm, d, dff, tiles_per_seq, n_tiles, final_norm):
    x_ref, mixer_w, ffn_w, o_ref = refs[0], refs[1:12], refs[12:18], refs[18]
    n_mixer_scr = 2 * (d // MXU_COLS) + 8
    mixer_scr, x1_scr, ffn_scr = refs[19:19 + n_mixer_scr], refs[19 + n_mixer_scr], refs[20 + n_mixer_scr:]
    i = pl.program_id(0)
    jm = lax.rem(jnp.minimum(i, n_tiles - 1), tiles_per_seq)
    mixer = _MixerStage(x_ref.at[0], x1_scr, jm, mixer_w, mixer_scr, tm=tm, d=d)
    ffn = _FfnStage(x1_scr, o_ref.at[0], ffn_w, ffn_scr, tm=tm, d=d, dff=dff, final_norm=final_norm)

    @pl.when(i == 0)
    def _():
        x1_scr[...] = jnp.zeros((tm, d), jnp.float32)
        ffn.zero_halos()

    @pl.when(jm == 0)
    def _():
        mixer.zero_halos()

    @pl.when(lax.rem(i + tiles_per_seq - 1, tiles_per_seq) == 0)
    def _():
        ffn.zero_halos()

    ffn.prologue()
    mixer.prologue()

    P = functools.partial
    nq, nk = mixer.n_chunks, ffn.n_chunks
    row_groups = range(0, tm, CONV_GROUP_ROWS)
    dot_cycles = tm
    conv_names = [f"conv{q}.{s}.{r0}" for q in range(nq) for s in range(STRIPS_PER_CHUNK)
                  for r0 in row_groups]
    pool_names = [f"pool{q}.{s}" for q in range(nq) for s in range(STRIPS_PER_CHUNK)]
    norm_names = [f"norm{r0}" for r0 in row_groups]

    def conv_in(q):
        return _Item(f"PV{q}", 2 * dot_cycles, P(mixer.project_conv_in, q), valu=tm // 2)

    def pool_in(q):
        return _Item(f"PP{q}", dot_cycles, P(mixer.project_pool_in, q))

    def gates(which, ns):
        return [_Item(f"G{which}.{n}", dot_cycles, P(mixer.gate_logits, which, n)) for n in ns]

    def up(k):
        return _Item(f"UP{k}", 2 * dot_cycles, P(ffn.up_project, k))

    def down(k):
        needs = [f"act{k}.{s}" for s in range(STRIPS_PER_CHUNK)]
        return _Item(f"DN{k}", dot_cycles, P(ffn.down_project, k), needs=needs, valu=tm // 4)

    plan = [conv_in(0), pool_in(0), up(0), conv_in(1), pool_in(1), up(1)]
    mixer_fill = [gates(0, (0, 1)), [conv_in(2), pool_in(2)], gates(0, (2, 3)),
                  [conv_in(3), pool_in(3)], gates(1, (0, 1)), gates(1, (2, 3))]
    for k in range(2, nk + 2):
        if mixer_fill:
            plan += mixer_fill.pop(0)
        if k < nk:
            plan.append(up(k))
        plan.append(down(k - 2))
    assert not mixer_fill
    plan += [_Item(f"M{n}", dot_cycles * 5 // 4, P(mixer.merge_chunk, n), needs=norm_names,
                   valu=3 * tm // 4) for n in range(nq)]
    plan += [_Item(f"O{n}", dot_cycles, P(mixer.out_chunk, n), valu=tm // 8) for n in range(nq)]

    pieces = []
    for k in range(nk):
        pieces += [_Item(f"act{k}.{s}", 0, P(ffn.activate_piece, k, s), needs=[f"UP{k}"], valu=3 * tm // 4)
                   for s in range(STRIPS_PER_CHUNK)]
    pieces.append(_Item("ffn_out", 0, ffn.epilogue, needs=[f"DN{nk - 1}"], valu=tm))
    for q in range(nq):
        for s in range(STRIPS_PER_CHUNK):
            pieces += [_Item(f"conv{q}.{s}.{r0}", 0, P(mixer.conv_piece, q, s, r0), needs=[f"PV{q}"],
                             valu=5 * CONV_GROUP_ROWS // 2) for r0 in row_groups]
            pieces.append(_Item(f"pool{q}.{s}", 0, P(mixer.pool_piece, q, s), needs=[f"PP{q}"], valu=tm // 4))
    pieces.append(_Item("carry", 0, mixer.carry_halos, needs=conv_names + pool_names, valu=tm // 8))
    pieces += [_Item(f"norm{r0}", 0, P(mixer.normalize_piece, r0), needs=conv_names + pool_names,
                     valu=7 * CONV_GROUP_ROWS // 2) for r0 in row_groups]
    _emit_interleaved(plan, pieces)


class _Layer:
    def __init__(self, stacked, layer):
        self.stacked, self.layer = stacked, layer


def _resident(arg):
    if isinstance(arg, _Layer):
        tail = arg.stacked.shape[1:]
        index = (arg.layer,) + (0,) * len(tail)
        return pl.BlockSpec((None,) + tail, lambda i: index, pipeline_mode=pl.Buffered(1))
    zeros = (0,) * arg.ndim
    return pl.BlockSpec(arg.shape, lambda i: zeros, pipeline_mode=pl.Buffered(1))


def _operand(arg):
    return arg.stacked if isinstance(arg, _Layer) else arg


def _layer(x, mixer_args, ffn_args, *, tm, final_norm):
    bsz, seq, d = x.shape
    dff = 2 * ffn_args[4].stacked.shape[1]
    tiles_per_seq = seq // tm
    n_tiles = bsz * tiles_per_seq

    def tile_index(t):
        return (t // tiles_per_seq, t % tiles_per_seq, 0)

    x_spec = pl.BlockSpec((1, tm, d), lambda i: tile_index(jnp.minimum(i, n_tiles - 1)))
    o_spec = pl.BlockSpec((1, tm, d), lambda i: tile_index(jnp.maximum(i - 1, 0)))
    args = tuple(mixer_args) + tuple(ffn_args)
    f32_tile = pltpu.VMEM((tm, d), jnp.float32)
    bf16_tile = pltpu.VMEM((tm, d), jnp.bfloat16)
    return pl.pallas_call(
        functools.partial(_layer_kernel, tm=tm, d=d, dff=dff, tiles_per_seq=tiles_per_seq,
                          n_tiles=n_tiles, final_norm=final_norm),
        grid=(n_tiles + 1,),
        in_specs=[x_spec] + [_resident(a) for a in args],
        out_specs=o_spec,
        out_shape=jax.ShapeDtypeStruct(x.shape, x.dtype),
        scratch_shapes=(
            [pltpu.VMEM((STRIPS_PER_CHUNK, CONV_HALO + tm, LANES), jnp.float32)] * (d // MXU_COLS)
            + [pltpu.VMEM((STRIPS_PER_CHUNK, POOL_HALO + tm, LANES), jnp.float32)] * (d // MXU_COLS)
            + [bf16_tile,
               f32_tile, f32_tile, f32_tile, f32_tile,
               bf16_tile, bf16_tile, bf16_tile,
               f32_tile]
            + [pltpu.VMEM((2 * STRIPS_PER_CHUNK, FFN_HALO + tm, LANES), jnp.float32)] * (dff // MXU_COLS)
            + [bf16_tile,
               pltpu.VMEM((tm, dff), jnp.bfloat16)]),
        compiler_params=pltpu.CompilerParams(
            dimension_semantics=("arbitrary",),
            vmem_limit_bytes=VMEM_LIMIT_BYTES),
        name="layer",
    )(x, *[_operand(a) for a in args])


def kernel(x, w_in, b_in, conv_dw_w, conv_dw_b, conv_ln_g, conv_ln_b, conv_w_out, pool_w, pool_scale, mix_w_out, norm_mix, ffn_w_up, ffn_dw_w, ffn_dw_b, ffn_w_down, norm_ffn, norm_final):
    depth = w_in.shape[0]
    row = lambda v: v.reshape(1, -1)
    tm = TIME_TILE
    win, cwo, mwo = _pack_bf16(w_in), _pack_bf16(conv_w_out), _pack_bf16(mix_w_out)
    pw = _pack_bf16(pool_w.reshape(depth, -1, pool_w.shape[-1]))
    wup, wdn = _pack_bf16(ffn_w_up), _pack_bf16(ffn_w_down)
    for l in range(depth):
        mixer_args = (row(norm_mix[l]), _Layer(win, l), row(b_in[l]), conv_dw_w[l],
                      row(conv_dw_b[l]), row(conv_ln_g[l]), row(conv_ln_b[l]), _Layer(cwo, l),
                      _Layer(pw, l), row(pool_scale[l]), _Layer(mwo, l))
        ffn_args = (row(norm_ffn[l]), _Layer(wup, l), ffn_dw_w[l], row(ffn_dw_b[l]),
                    _Layer(wdn, l), row(norm_final))
        x = _layer(x, mixer_args, ffn_args, tm=tm, final_norm=(l == depth - 1))
    return x
```

```python
import functools
import math

import jax
import jax.numpy as jnp
from jax import lax
from jax.experimental import pallas as pl
from jax.experimental.pallas import tpu as pltpu

LANES = 128
SUBLANES = 8
MXU_COLS = 256
VMEM_LIMIT_BYTES = 60 * 1024 * 1024

EPS = 1e-6
CONV_KERNEL = 31
POOL_WINDOWS = (2, 4, 8, 16)
FFN_KERNEL = 3

CONV_HALO = 32
POOL_HALO = 16
FFN_HALO = 8
PACKED_ROWS = 16
CONV_GROUP_ROWS = 64
PACK_BLOCK_BYTES = 6 * 1024 * 1024
TIME_TILE = 256
STRIPS_PER_CHUNK = MXU_COLS // LANES

GELU_C = math.sqrt(2.0 / math.pi)


def _rmsnorm(x, g):
    ms = jnp.mean(x * x, axis=-1, keepdims=True)
    return x * lax.rsqrt(ms + EPS) * g


def _sigmoid(x):
    return 1.0 / (1.0 + jnp.exp(-x))


def _gelu_tanh(x):
    hx = 0.5 * x
    return hx + hx * jnp.tanh(x * (GELU_C + (0.044715 * GELU_C) * (x * x)))


def _dot(a, b):
    return jnp.dot(a, b, preferred_element_type=jnp.float32)


def _pack_kernel(w_ref, o_ref):
    o_ref[...] = pltpu.bitcast(w_ref[...].astype(jnp.bfloat16), jnp.int32)


def _pack_bf16(w):
    n_layers, k, n = w.shape
    bk = max(b for b in range(PACKED_ROWS, k + 1, PACKED_ROWS)
             if k % b == 0 and b * n * 4 <= PACK_BLOCK_BYTES)
    return pl.pallas_call(
        _pack_kernel,
        grid=(n_layers, k // bk),
        in_specs=[pl.BlockSpec((None, bk, n), lambda l, i: (l, i, 0))],
        out_specs=pl.BlockSpec((None, bk // 2, n), lambda l, i: (l, i, 0)),
        out_shape=jax.ShapeDtypeStruct((n_layers, k // 2, n), jnp.int32),
        compiler_params=pltpu.CompilerParams(
            dimension_semantics=("arbitrary", "arbitrary"),
            vmem_limit_bytes=VMEM_LIMIT_BYTES),
        name="pack_bf16",
    )(w)


def _unpack(packed):
    return pltpu.bitcast(packed, jnp.bfloat16)


def _vreg_row(ref, k, cs):
    return jnp.broadcast_to(ref[k:k + 1, cs], (SUBLANES, LANES))


class _MixerStage:
    def __init__(self, x_tile, out_tile, j, weights, scratch, *, tm, d):
        (self.nrm_ref, self.win_ref, self.bin_ref, self.dww_ref, self.dwb_ref, self.lng_ref,
         self.lnb_ref, self.cwo_ref, self.pw_ref, self.ps_ref, self.mwo_ref) = weights
        self.n_chunks = d // MXU_COLS
        assert self.n_chunks == len(POOL_WINDOWS)
        self.u_ext, scratch = scratch[:self.n_chunks], scratch[self.n_chunks:]
        self.p_ext, scratch = scratch[:self.n_chunks], scratch[self.n_chunks:]
        (self.h_scr, self.conv_scr, self.pooled_scr, self.ga_scr,
         self.gb_scr, self.act_scr, self.pooled_bf, self.merged_scr) = scratch
        self.x_tile, self.out_tile, self.j, self.tm, self.d = x_tile, out_tile, j, tm, d

    def zero_halos(self):
        for q in range(self.n_chunks):
            self.u_ext[q][:, 0:CONV_HALO, :] = jnp.zeros((STRIPS_PER_CHUNK, CONV_HALO, LANES), jnp.float32)
            self.p_ext[q][:, 0:POOL_HALO, :] = jnp.zeros((STRIPS_PER_CHUNK, POOL_HALO, LANES), jnp.float32)

    def prologue(self):
        self.h_scr[...] = _rmsnorm(self.x_tile[...], self.nrm_ref[...]).astype(jnp.bfloat16)

    def _proj(self, col0, cols):
        sl = slice(col0 + cols.start, col0 + cols.stop)
        return _dot(self.h_scr[...], _unpack(self.win_ref[:, sl])) + self.bin_ref[:, sl]

    def project_conv_in(self, q):
        tm, d = self.tm, self.d
        cols = slice(q * MXU_COLS, (q + 1) * MXU_COLS)
        u = self._proj(0, cols) * _sigmoid(self._proj(d, cols))
        for s in range(STRIPS_PER_CHUNK):
            self.u_ext[q][s, CONV_HALO:CONV_HALO + tm, :] = u[:, s * LANES:(s + 1) * LANES]

    def project_pool_in(self, q):
        tm, d = self.tm, self.d
        pin = self._proj(2 * d, slice(q * MXU_COLS, (q + 1) * MXU_COLS))
        for s in range(STRIPS_PER_CHUNK):
            self.p_ext[q][s, POOL_HALO:POOL_HALO + tm, :] = pin[:, s * LANES:(s + 1) * LANES]

    def conv_piece(self, q, s, r0):
        conv_base = CONV_HALO - (CONV_KERNEL - 1)
        c = q * STRIPS_PER_CHUNK + s
        cs = slice(c * LANES, (c + 1) * LANES)
        ws = [_vreg_row(self.dww_ref, k, cs) for k in range(CONV_KERNEL)]
        blocks = range(r0, r0 + CONV_GROUP_ROWS, SUBLANES)
        accs = {r: _vreg_row(self.dwb_ref, 0, cs) for r in blocks}
        for lo in range(r0 + conv_base, r0 + conv_base + CONV_GROUP_ROWS + CONV_KERNEL - 1):
            taps = [(r, lo - conv_base - r) for r in blocks
                    if 0 <= lo - conv_base - r < CONV_KERNEL]
            win = self.u_ext[q][s, lo:lo + SUBLANES, :]
            for r, k in taps:
                accs[r] = accs[r] + ws[k] * win
        for r in blocks:
            self.conv_scr[r:r + SUBLANES, cs] = accs[r]

    def pool_piece(self, q, s):
        tm = self.tm
        row_iota = lax.broadcasted_iota(jnp.int32, (SUBLANES, LANES), 0)
        window = POOL_WINDOWS[q]
        c = q * STRIPS_PER_CHUNK + s
        cs = slice(c * LANES, (c + 1) * LANES)
        for r in range(0, tm, SUBLANES):
            tok = self.p_ext[q][s, r + POOL_HALO:r + POOL_HALO + SUBLANES, :]
            acc = tok
            for back in range(1, window):
                lo = r + POOL_HALO - back
                acc = acc + self.p_ext[q][s, lo:lo + SUBLANES, :]
            pos1 = (self.j * tm + (r + 1) + row_iota).astype(jnp.float32)
            cnt = jnp.minimum(pos1, float(window))
            self.pooled_scr[r:r + SUBLANES, cs] = acc / cnt - tok

    def gate_logits(self, which, n):
        col0, dst = ((3 * self.d, self.ga_scr), (4 * self.d, self.gb_scr))[which]
        cols = slice(n * MXU_COLS, (n + 1) * MXU_COLS)
        dst[:, cols] = _dot(self.h_scr[...],
                            _unpack(self.win_ref[:, col0 + cols.start:col0 + cols.stop]))

    def _gate(self, which, cols):
        col0, src = ((3 * self.d, self.ga_scr), (4 * self.d, self.gb_scr))[which]
        return _sigmoid(src[:, cols] + self.bin_ref[:, col0 + cols.start:col0 + cols.stop])

    def carry_halos(self):
        tm = self.tm
        for q in range(self.n_chunks):
            self.u_ext[q][:, 0:CONV_HALO, :] = self.u_ext[q][:, tm:tm + CONV_HALO, :]
            self.p_ext[q][:, 0:POOL_HALO, :] = self.p_ext[q][:, tm:tm + POOL_HALO, :]

    def normalize_piece(self, r0):
        rows = slice(r0, r0 + CONV_GROUP_ROWS)
        y = self.conv_scr[rows, :]
        mu = jnp.mean(y, axis=-1, keepdims=True)
        yc = y - mu
        var = jnp.mean(yc * yc, axis=-1, keepdims=True)
        yn = yc * lax.rsqrt(var + EPS) * self.lng_ref[...] + self.lnb_ref[...]
        self.act_scr[rows, :] = (yn * _sigmoid(yn)).astype(jnp.bfloat16)
        self.pooled_bf[rows, :] = self.pooled_scr[rows, :].astype(jnp.bfloat16)

    def merge_chunk(self, n):
        cols = slice(n * MXU_COLS, (n + 1) * MXU_COLS)
        ya = _dot(self.act_scr[...], _unpack(self.cwo_ref[:, cols]))
        group_rows = slice(cols.start // 2, cols.stop // 2)
        yb = _dot(self.pooled_bf[:, cols], _unpack(self.pw_ref[group_rows, :])) * self.ps_ref[:, cols]
        merged = self._gate(0, cols) * ya + self._gate(1, cols) * yb
        self.merged_scr[:, cols] = merged.astype(jnp.bfloat16)

    def out_chunk(self, n):
        cols = slice(n * MXU_COLS, (n + 1) * MXU_COLS)
        self.out_tile[:, cols] = self.x_tile[:, cols] + _dot(
            self.merged_scr[...], _unpack(self.mwo_ref[:, cols]))


class _FfnStage:
    def __init__(self, x_tile, out_tile, weights, scratch, *, tm, d, dff, final_norm):
        self.nrm_ref, self.wup_ref, self.dww_ref, self.dwb_ref, self.wdn_ref, self.nf_ref = weights
        self.n_chunks = dff // MXU_COLS
        self.u_ext, (self.h_scr, self.act_scr) = scratch[:self.n_chunks], scratch[self.n_chunks:]
        self.x_tile, self.out_tile = x_tile, out_tile
        self.tm, self.d, self.dff, self.final_norm = tm, d, dff, final_norm
        self.y = None

    def zero_halos(self):
        for q in range(self.n_chunks):
            self.u_ext[q][:, 0:FFN_HALO, :] = jnp.zeros((2 * STRIPS_PER_CHUNK, FFN_HALO, LANES), jnp.float32)

    def prologue(self):
        x = self.x_tile[...]
        self.h_scr[...] = _rmsnorm(x, self.nrm_ref[...]).astype(jnp.bfloat16)
        self.y = x

    def _conv3(self, q, local, r):
        base = FFN_HALO - (FFN_KERNEL - 1)
        is_gate, s = divmod(local, STRIPS_PER_CHUNK)
        col0 = is_gate * self.dff + (q * STRIPS_PER_CHUNK + s) * LANES
        cs = slice(col0, col0 + LANES)
        strip = self.u_ext[q]
        acc = (_vreg_row(self.dww_ref, 0, cs) * strip[local, r + base:r + base + SUBLANES, :]
               + _vreg_row(self.dwb_ref, 0, cs))
        for k in range(1, FFN_KERNEL):
            lo = r + base + k
            acc = acc + _vreg_row(self.dww_ref, k, cs) * strip[local, lo:lo + SUBLANES, :]
        return acc

    def up_project(self, q):
        tm, dff = self.tm, self.dff
        cols = slice(q * MXU_COLS, (q + 1) * MXU_COLS)
        a = _dot(self.h_scr[...], _unpack(self.wup_ref[:, cols]))
        g = _dot(self.h_scr[...], _unpack(self.wup_ref[:, dff + cols.start:dff + cols.stop]))
        for s in range(STRIPS_PER_CHUNK):
            ls = slice(s * LANES, (s + 1) * LANES)
            self.u_ext[q][s, FFN_HALO:FFN_HALO + tm, :] = a[:, ls]
            self.u_ext[q][STRIPS_PER_CHUNK + s, FFN_HALO:FFN_HALO + tm, :] = g[:, ls]

    def activate_piece(self, q, s):
        tm = self.tm
        c = q * STRIPS_PER_CHUNK + s
        for r in range(0, tm, PACKED_ROWS):
            rows = [_gelu_tanh(self._conv3(q, s, r + rr))
                    * self._conv3(q, STRIPS_PER_CHUNK + s, r + rr)
                    for rr in range(0, PACKED_ROWS, SUBLANES)]
            self.act_scr[r:r + PACKED_ROWS, c * LANES:(c + 1) * LANES] = (
                jnp.concatenate(rows, axis=0).astype(jnp.bfloat16))
        for local in (s, STRIPS_PER_CHUNK + s):
            self.u_ext[q][local, 0:FFN_HALO, :] = self.u_ext[q][local, tm:tm + FFN_HALO, :]

    def down_project(self, q):
        cols = slice(q * MXU_COLS, (q + 1) * MXU_COLS)
        packed_rows = slice(cols.start // 2, cols.stop // 2)
        self.y = self.y + _dot(self.act_scr[:, cols], _unpack(self.wdn_ref[packed_rows, :]))

    def epilogue(self):
        y = self.y
        if self.final_norm:
            y = _rmsnorm(y, self.nf_ref[...])
        self.out_tile[...] = y


class _Item:
    def __init__(self, name, mxu, emit, needs=(), valu=0, urgent=False):
        self.name, self.mxu, self.emit, self.needs, self.valu = name, mxu, emit, tuple(needs), valu
        self.urgent = urgent


def _emit_interleaved(plan, pieces):
    total_mxu = sum(it.mxu for it in plan)
    total_valu = sum(it.valu for it in plan) + sum(it.valu for it in pieces)
    done, pending = set(), list(pieces)
    spent = {"mxu": 0, "valu": 0}

    def emit(item):
        assert all(n in done for n in item.needs), (item.name, item.needs)
        item.emit()
        done.add(item.name)
        spent["mxu"] += item.mxu
        spent["valu"] += item.valu

    def emit_piece(name):
        piece = next(p for p in pending if p.name == name)
        pending.remove(piece)
        for need in piece.needs:
            if need not in done:
                emit_piece(need)
        emit(piece)

    for item in plan:
        for name in item.needs:
            if name not in done:
                emit_piece(name)
        emit(item)
        while True:
            ready = [p for p in pending if all(n in done for n in p.needs)]
            urgent = [p for p in ready if p.urgent]
            behind = spent["valu"] * total_mxu < spent["mxu"] * total_valu
            if urgent:
                emit_piece(urgent[0].name)
            elif ready and behind:
                emit_piece(ready[0].name)
            else:
                break
    for piece in list(pending):
        emit_piece(piece.name)


def _layer_kernel(*refs, tm, d, dff, tiles_per_seq, n_tiles, final_norm):
    x_ref, mixer_w, ffn_w, o_ref = refs[0], refs[1:12], refs[12:18], refs[18]
    n_mixer_scr = 2 * (d // MXU_COLS) + 8
    mixer_scr, x1_scr, ffn_scr = refs[19:19 + n_mixer_scr], refs[19 + n_mixer_scr], refs[20 + n_mixer_scr:]
    i = pl.program_id(0)
    jm = lax.rem(jnp.minimum(i, n_tiles - 1), tiles_per_seq)
    mixer = _MixerStage(x_ref.at[0], x1_scr, jm, mixer_w, mixer_scr, tm=tm, d=d)
    ffn = _FfnStage(x1_scr, o_ref.at[0], ffn_w, ffn_scr, tm=tm, d=d, dff=dff, final_norm=final_norm)

    @pl.when(i == 0)
    def _():
        x1_scr[...] = jnp.zeros((tm, d), jnp.float32)
        ffn.zero_halos()

    @pl.when(jm == 0)
    def _():
        mixer.zero_halos()

    @pl.when(lax.rem(i + tiles_per_seq - 1, tiles_per_seq) == 0)
    def _():
        ffn.zero_halos()

    ffn.prologue()
    mixer.prologue()

    P = functools.partial
    nq, nk = mixer.n_chunks, ffn.n_chunks
    row_groups = range(0, tm, CONV_GROUP_ROWS)
    dot_cycles = tm
    conv_names = [f"conv{q}.{s}.{r0}" for q in range(nq) for s in range(STRIPS_PER_CHUNK)
                  for r0 in row_groups]
    pool_names = [f"pool{q}.{s}" for q in range(nq) for s in range(STRIPS_PER_CHUNK)]
    norm_names = [f"norm{r0}" for r0 in row_groups]

    def conv_in(q):
        return _Item(f"PV{q}", 2 * dot_cycles, P(mixer.project_conv_in, q), valu=tm // 2)

    def pool_in(q):
        return _Item(f"PP{q}", dot_cycles, P(mixer.project_pool_in, q))

    def gates(which, ns):
        return [_Item(f"G{which}.{n}", dot_cycles, P(mixer.gate_logits, which, n)) for n in ns]

    def up(k):
        return _Item(f"UP{k}", 2 * dot_cycles, P(ffn.up_project, k))

    def down(k):
        needs = [f"act{k}.{s}" for s in range(STRIPS_PER_CHUNK)]
        return _Item(f"DN{k}", dot_cycles, P(ffn.down_project, k), needs=needs, valu=tm // 4)

    plan = [conv_in(0), pool_in(0), up(0), conv_in(1), pool_in(1), up(1)]
    plan += gates(0, (0, 1)) + [up(2), conv_in(2), pool_in(2), up(3)] + gates(0, (2, 3))
    plan += [up(4), conv_in(3), pool_in(3)] + gates(1, (0, 1, 2, 3))
    tail = ([_Item(f"M{n}", dot_cycles * 5 // 4, P(mixer.merge_chunk, n), needs=norm_names,
                   valu=3 * tm // 4) for n in range(nq)]
            + [_Item(f"O{n}", dot_cycles, P(mixer.out_chunk, n), valu=tm // 8) for n in range(nq)])
    first_late_up = 5
    for k in range(nk):
        if first_late_up + k < nk:
            plan.append(up(first_late_up + k))
        elif tail:
            plan.append(tail.pop(0))
        plan.append(down(k))
    plan += tail

    pieces = []
    for k in range(nk):
        pieces += [_Item(f"act{k}.{s}", 0, P(ffn.activate_piece, k, s), needs=[f"UP{k}"], valu=3 * tm // 4,
                         urgent=True) for s in range(STRIPS_PER_CHUNK)]
    pieces.append(_Item("ffn_out", 0, ffn.epilogue, needs=[f"DN{nk - 1}"], valu=tm))
    for q in range(nq):
        for s in range(STRIPS_PER_CHUNK):
            pieces += [_Item(f"conv{q}.{s}.{r0}", 0, P(mixer.conv_piece, q, s, r0), needs=[f"PV{q}"],
                             valu=5 * CONV_GROUP_ROWS // 2) for r0 in row_groups]
            pieces.append(_Item(f"pool{q}.{s}", 0, P(mixer.pool_piece, q, s), needs=[f"PP{q}"], valu=tm // 4))
    pieces.append(_Item("carry", 0, mixer.carry_halos, needs=conv_names + pool_names, valu=tm // 8))
    pieces += [_Item(f"norm{r0}", 0, P(mixer.normalize_piece, r0), needs=conv_names + pool_names,
                     valu=7 * CONV_GROUP_ROWS // 2) for r0 in row_groups]
    _emit_interleaved(plan, pieces)


class _Layer:
    def __init__(self, stacked, layer):
        self.stacked, self.layer = stacked, layer


def _resident(arg):
    if isinstance(arg, _Layer):
        tail = arg.stacked.shape[1:]
        index = (arg.layer,) + (0,) * len(tail)
        return pl.BlockSpec((None,) + tail, lambda i: index, pipeline_mode=pl.Buffered(1))
    zeros = (0,) * arg.ndim
    return pl.BlockSpec(arg.shape, lambda i: zeros, pipeline_mode=pl.Buffered(1))


def _operand(arg):
    return arg.stacked if isinstance(arg, _Layer) else arg


def _layer(x, mixer_args, ffn_args, *, tm, final_norm):
    bsz, seq, d = x.shape
    dff = 2 * ffn_args[4].stacked.shape[1]
    tiles_per_seq = seq // tm
    n_tiles = bsz * tiles_per_seq

    def tile_index(t):
        return (t // tiles_per_seq, t % tiles_per_seq, 0)

    x_spec = pl.BlockSpec((1, tm, d), lambda i: tile_index(jnp.minimum(i, n_tiles - 1)))
    o_spec = pl.BlockSpec((1, tm, d), lambda i: tile_index(jnp.maximum(i - 1, 0)))
    args = tuple(mixer_args) + tuple(ffn_args)
    f32_tile = pltpu.VMEM((tm, d), jnp.float32)
    bf16_tile = pltpu.VMEM((tm, d), jnp.bfloat16)
    return pl.pallas_call(
        functools.partial(_layer_kernel, tm=tm, d=d, dff=dff, tiles_per_seq=tiles_per_seq,
                          n_tiles=n_tiles, final_norm=final_norm),
        grid=(n_tiles + 1,),
        in_specs=[x_spec] + [_resident(a) for a in args],
        out_specs=o_spec,
        out_shape=jax.ShapeDtypeStruct(x.shape, x.dtype),
        scratch_shapes=(
            [pltpu.VMEM((STRIPS_PER_CHUNK, CONV_HALO + tm, LANES), jnp.float32)] * (d // MXU_COLS)
            + [pltpu.VMEM((STRIPS_PER_CHUNK, POOL_HALO + tm, LANES), jnp.float32)] * (d // MXU_COLS)
            + [bf16_tile,
               f32_tile, f32_tile, f32_tile, f32_tile,
               bf16_tile, bf16_tile, bf16_tile,
               f32_tile]
            + [pltpu.VMEM((2 * STRIPS_PER_CHUNK, FFN_HALO + tm, LANES), jnp.float32)] * (dff // MXU_COLS)
            + [bf16_tile,
               pltpu.VMEM((tm, dff), jnp.bfloat16)]),
        compiler_params=pltpu.CompilerParams(
            dimension_semantics=("arbitrary",),
            vmem_limit_bytes=VMEM_LIMIT_BYTES),
        name="layer",
    )(x, *[_operand(a) for a in args])


def kernel(x, w_in, b_in, conv_dw_w, conv_dw_b, conv_ln_g, conv_ln_b, conv_w_out, pool_w, pool_scale, mix_w_out, norm_mix, ffn_w_up, ffn_dw_w, ffn_dw_b, ffn_w_down, norm_ffn, norm_final):
    depth = w_in.shape[0]
    row = lambda v: v.reshape(1, -1)
    tm = TIME_TILE
    win, cwo, mwo = _pack_bf16(w_in), _pack_bf16(conv_w_out), _pack_bf16(mix_w_out)
    pw = _pack_bf16(pool_w.reshape(depth, -1, pool_w.shape[-1]))
    wup, wdn = _pack_bf16(ffn_w_up), _pack_bf16(ffn_w_down)
    for l in range(depth):
        mixer_args = (row(norm_mix[l]), _Layer(win, l), row(b_in[l]), conv_dw_w[l],
                      row(conv_dw_b[l]), row(conv_ln_g[l]), row(conv_ln_b[l]), _Layer(cwo, l),
                      _Layer(pw, l), row(pool_scale[l]), _Layer(mwo, l))
        ffn_args = (row(norm_ffn[l]), _Layer(wup, l), ffn_dw_w[l], row(ffn_dw_b[l]),
                    _Layer(wdn, l), row(norm_final))
        x = _layer(x, mixer_args, ffn_args, tm=tm, final_norm=(l == depth - 1))
    return x
```

```python
import functools
import math

import jax
import jax.numpy as jnp
from jax import lax
from jax.experimental import pallas as pl
from jax.experimental.pallas import tpu as pltpu

LANES = 128
SUBLANES = 8
MXU_COLS = 256
VMEM_LIMIT_BYTES = 60 * 1024 * 1024

EPS = 1e-6
CONV_KERNEL = 31
POOL_WINDOWS = (2, 4, 8, 16)
FFN_KERNEL = 3

CONV_HALO = 32
POOL_HALO = 16
FFN_HALO = 8
PACKED_ROWS = 16
CONV_GROUP_ROWS = 64
PACK_BLOCK_BYTES = 6 * 1024 * 1024
TIME_TILE = 256
STRIPS_PER_CHUNK = MXU_COLS // LANES

GELU_C = math.sqrt(2.0 / math.pi)


def _rmsnorm(x, g):
    ms = jnp.mean(x * x, axis=-1, keepdims=True)
    return x * lax.rsqrt(ms + EPS) * g


def _sigmoid(x):
    return 0.5 + 0.5 * jnp.tanh(0.5 * x)


def _gelu_tanh(x):
    hx = 0.5 * x
    return hx + hx * jnp.tanh(x * (GELU_C + (0.044715 * GELU_C) * (x * x)))


def _dot(a, b):
    return jnp.dot(a, b, preferred_element_type=jnp.float32)


def _pack_kernel(w_ref, o_ref):
    o_ref[...] = pltpu.bitcast(w_ref[...].astype(jnp.bfloat16), jnp.int32)


def _pack_bf16(w):
    n_layers, k, n = w.shape
    bk = max(b for b in range(PACKED_ROWS, k + 1, PACKED_ROWS)
             if k % b == 0 and b * n * 4 <= PACK_BLOCK_BYTES)
    return pl.pallas_call(
        _pack_kernel,
        grid=(n_layers, k // bk),
        in_specs=[pl.BlockSpec((None, bk, n), lambda l, i: (l, i, 0))],
        out_specs=pl.BlockSpec((None, bk // 2, n), lambda l, i: (l, i, 0)),
        out_shape=jax.ShapeDtypeStruct((n_layers, k // 2, n), jnp.int32),
        compiler_params=pltpu.CompilerParams(
            dimension_semantics=("arbitrary", "arbitrary"),
            vmem_limit_bytes=VMEM_LIMIT_BYTES),
        name="pack_bf16",
    )(w)


def _unpack(packed):
    return pltpu.bitcast(packed, jnp.bfloat16)


def _vreg_row(ref, k, cs):
    return jnp.broadcast_to(ref[k:k + 1, cs], (SUBLANES, LANES))


class _MixerStage:
    def __init__(self, x_tile, out_tile, j, weights, scratch, *, tm, d):
        (self.nrm_ref, self.win_ref, self.bin_ref, self.dww_ref, self.dwb_ref, self.lng_ref,
         self.lnb_ref, self.cwo_ref, self.pw_ref, self.ps_ref, self.mwo_ref) = weights
        self.n_chunks = d // MXU_COLS
        assert self.n_chunks == len(POOL_WINDOWS)
        self.u_ext, scratch = scratch[:self.n_chunks], scratch[self.n_chunks:]
        self.p_ext, scratch = scratch[:self.n_chunks], scratch[self.n_chunks:]
        (self.h_scr, self.conv_scr, self.pooled_scr, self.ga_scr,
         self.gb_scr, self.act_scr, self.pooled_bf, self.merged_scr) = scratch
        self.x_tile, self.out_tile, self.j, self.tm, self.d = x_tile, out_tile, j, tm, d

    def zero_halos(self):
        for q in range(self.n_chunks):
            self.u_ext[q][:, 0:CONV_HALO, :] = jnp.zeros((STRIPS_PER_CHUNK, CONV_HALO, LANES), jnp.float32)
            self.p_ext[q][:, 0:POOL_HALO, :] = jnp.zeros((STRIPS_PER_CHUNK, POOL_HALO, LANES), jnp.float32)

    def normalize_input(self, x_tile):
        self.h_scr[...] = _rmsnorm(x_tile[...], self.nrm_ref[...]).astype(jnp.bfloat16)

    def _proj(self, col0, cols):
        sl = slice(col0 + cols.start, col0 + cols.stop)
        return _dot(self.h_scr[...], _unpack(self.win_ref[:, sl])) + self.bin_ref[:, sl]

    def project_conv_in(self, q):
        tm, d = self.tm, self.d
        cols = slice(q * MXU_COLS, (q + 1) * MXU_COLS)
        u = self._proj(0, cols) * _sigmoid(self._proj(d, cols))
        for s in range(STRIPS_PER_CHUNK):
            self.u_ext[q][s, CONV_HALO:CONV_HALO + tm, :] = u[:, s * LANES:(s + 1) * LANES]

    def project_pool_in(self, q):
        tm, d = self.tm, self.d
        pin = self._proj(2 * d, slice(q * MXU_COLS, (q + 1) * MXU_COLS))
        for s in range(STRIPS_PER_CHUNK):
            self.p_ext[q][s, POOL_HALO:POOL_HALO + tm, :] = pin[:, s * LANES:(s + 1) * LANES]

    def conv_piece(self, q, s, r0):
        conv_base = CONV_HALO - (CONV_KERNEL - 1)
        c = q * STRIPS_PER_CHUNK + s
        cs = slice(c * LANES, (c + 1) * LANES)
        ws = [_vreg_row(self.dww_ref, k, cs) for k in range(CONV_KERNEL)]
        blocks = range(r0, r0 + CONV_GROUP_ROWS, SUBLANES)
        accs = {r: _vreg_row(self.dwb_ref, 0, cs) for r in blocks}
        for lo in range(r0 + conv_base, r0 + conv_base + CONV_GROUP_ROWS + CONV_KERNEL - 1):
            taps = [(r, lo - conv_base - r) for r in blocks
                    if 0 <= lo - conv_base - r < CONV_KERNEL]
            win = self.u_ext[q][s, lo:lo + SUBLANES, :]
            for r, k in taps:
                accs[r] = accs[r] + ws[k] * win
        for r in blocks:
            self.conv_scr[r:r + SUBLANES, cs] = accs[r]

    def pool_piece(self, q, s):
        tm = self.tm
        row_iota = lax.broadcasted_iota(jnp.int32, (SUBLANES, LANES), 0)
        window = POOL_WINDOWS[q]
        c = q * STRIPS_PER_CHUNK + s
        cs = slice(c * LANES, (c + 1) * LANES)
        for r in range(0, tm, SUBLANES):
            tok = self.p_ext[q][s, r + POOL_HALO:r + POOL_HALO + SUBLANES, :]
            acc = tok
            for back in range(1, window):
                lo = r + POOL_HALO - back
                acc = acc + self.p_ext[q][s, lo:lo + SUBLANES, :]
            pos1 = (self.j * tm + (r + 1) + row_iota).astype(jnp.float32)
            cnt = jnp.minimum(pos1, float(window))
            self.pooled_scr[r:r + SUBLANES, cs] = acc / cnt - tok

    def gate_logits(self, which, n):
        col0, dst = ((3 * self.d, self.ga_scr), (4 * self.d, self.gb_scr))[which]
        cols = slice(n * MXU_COLS, (n + 1) * MXU_COLS)
        dst[:, cols] = _dot(self.h_scr[...],
                            _unpack(self.win_ref[:, col0 + cols.start:col0 + cols.stop]))

    def _gate(self, which, cols):
        col0, src = ((3 * self.d, self.ga_scr), (4 * self.d, self.gb_scr))[which]
        return _sigmoid(src[:, cols] + self.bin_ref[:, col0 + cols.start:col0 + cols.stop])

    def carry_halos(self):
        tm = self.tm
        for q in range(self.n_chunks):
            self.u_ext[q][:, 0:CONV_HALO, :] = self.u_ext[q][:, tm:tm + CONV_HALO, :]
            self.p_ext[q][:, 0:POOL_HALO, :] = self.p_ext[q][:, tm:tm + POOL_HALO, :]

    def normalize_piece(self, r0):
        rows = slice(r0, r0 + CONV_GROUP_ROWS)
        y = self.conv_scr[rows, :]
        mu = jnp.mean(y, axis=-1, keepdims=True)
        yc = y - mu
        var = jnp.mean(yc * yc, axis=-1, keepdims=True)
        yn = yc * lax.rsqrt(var + EPS) * self.lng_ref[...] + self.lnb_ref[...]
        self.act_scr[rows, :] = (yn * _sigmoid(yn)).astype(jnp.bfloat16)
        self.pooled_bf[rows, :] = self.pooled_scr[rows, :].astype(jnp.bfloat16)

    def merge_chunk(self, n):
        cols = slice(n * MXU_COLS, (n + 1) * MXU_COLS)
        ya = _dot(self.act_scr[...], _unpack(self.cwo_ref[:, cols]))
        group_rows = slice(cols.start // 2, cols.stop // 2)
        yb = _dot(self.pooled_bf[:, cols], _unpack(self.pw_ref[group_rows, :])) * self.ps_ref[:, cols]
        merged = self._gate(0, cols) * ya + self._gate(1, cols) * yb
        self.merged_scr[:, cols] = merged.astype(jnp.bfloat16)

    def out_chunk(self, n):
        cols = slice(n * MXU_COLS, (n + 1) * MXU_COLS)
        self.out_tile[:, cols] = self.x_tile[:, cols] + _dot(
            self.merged_scr[...], _unpack(self.mwo_ref[:, cols]))


class _FfnStage:
    def __init__(self, x_tile, out_tile, weights, scratch, *, tm, d, dff, final_norm):
        self.nrm_ref, self.wup_ref, self.dww_ref, self.dwb_ref, self.wdn_ref, self.nf_ref = weights
        self.n_chunks = dff // MXU_COLS
        self.u_ext, (self.h_scr, self.act_scr) = scratch[:self.n_chunks], scratch[self.n_chunks:]
        self.x_tile, self.out_tile = x_tile, out_tile
        self.tm, self.d, self.dff, self.final_norm = tm, d, dff, final_norm
        self.y = None

    def zero_halos(self):
        for q in range(self.n_chunks):
            self.u_ext[q][:, 0:FFN_HALO, :] = jnp.zeros((2 * STRIPS_PER_CHUNK, FFN_HALO, LANES), jnp.float32)

    def normalize_input(self):
        self.h_scr[...] = _rmsnorm(self.x_tile[...], self.nrm_ref[...]).astype(jnp.bfloat16)

    def begin(self):
        self.y = self.x_tile[...]

    def _conv3(self, q, local, r):
        base = FFN_HALO - (FFN_KERNEL - 1)
        is_gate, s = divmod(local, STRIPS_PER_CHUNK)
        col0 = is_gate * self.dff + (q * STRIPS_PER_CHUNK + s) * LANES
        cs = slice(col0, col0 + LANES)
        strip = self.u_ext[q]
        acc = (_vreg_row(self.dww_ref, 0, cs) * strip[local, r + base:r + base + SUBLANES, :]
               + _vreg_row(self.dwb_ref, 0, cs))
        for k in range(1, FFN_KERNEL):
            lo = r + base + k
            acc = acc + _vreg_row(self.dww_ref, k, cs) * strip[local, lo:lo + SUBLANES, :]
        return acc

    def up_project(self, q):
        tm, dff = self.tm, self.dff
        cols = slice(q * MXU_COLS, (q + 1) * MXU_COLS)
        a = _dot(self.h_scr[...], _unpack(self.wup_ref[:, cols]))
        g = _dot(self.h_scr[...], _unpack(self.wup_ref[:, dff + cols.start:dff + cols.stop]))
        for s in range(STRIPS_PER_CHUNK):
            ls = slice(s * LANES, (s + 1) * LANES)
            self.u_ext[q][s, FFN_HALO:FFN_HALO + tm, :] = a[:, ls]
            self.u_ext[q][STRIPS_PER_CHUNK + s, FFN_HALO:FFN_HALO + tm, :] = g[:, ls]

    def activate_piece(self, q, s):
        tm = self.tm
        c = q * STRIPS_PER_CHUNK + s
        for r in range(0, tm, PACKED_ROWS):
            rows = [_gelu_tanh(self._conv3(q, s, r + rr))
                    * self._conv3(q, STRIPS_PER_CHUNK + s, r + rr)
                    for rr in range(0, PACKED_ROWS, SUBLANES)]
            self.act_scr[r:r + PACKED_ROWS, c * LANES:(c + 1) * LANES] = (
                jnp.concatenate(rows, axis=0).astype(jnp.bfloat16))
        for local in (s, STRIPS_PER_CHUNK + s):
            self.u_ext[q][local, 0:FFN_HALO, :] = self.u_ext[q][local, tm:tm + FFN_HALO, :]

    def down_project(self, q):
        cols = slice(q * MXU_COLS, (q + 1) * MXU_COLS)
        packed_rows = slice(cols.start // 2, cols.stop // 2)
        self.y = self.y + _dot(self.act_scr[:, cols], _unpack(self.wdn_ref[packed_rows, :]))

    def epilogue(self):
        y = self.y
        if self.final_norm:
            y = _rmsnorm(y, self.nf_ref[...])
        self.out_tile[...] = y


class _Item:
    def __init__(self, name, mxu, emit, needs=(), valu=0, urgent=False):
        self.name, self.mxu, self.emit, self.needs, self.valu = name, mxu, emit, tuple(needs), valu
        self.urgent = urgent


def _emit_interleaved(plan, pieces):
    total_mxu = sum(it.mxu for it in plan)
    total_valu = sum(it.valu for it in plan) + sum(it.valu for it in pieces)
    done, pending = set(), list(pieces)
    spent = {"mxu": 0, "valu": 0}

    def emit(item):
        assert all(n in done for n in item.needs), (item.name, item.needs)
        item.emit()
        done.add(item.name)
        spent["mxu"] += item.mxu
        spent["valu"] += item.valu

    def emit_piece(name):
        piece = next(p for p in pending if p.name == name)
        pending.remove(piece)
        for need in piece.needs:
            if need not in done:
                emit_piece(need)
        emit(piece)

    for item in plan:
        for name in item.needs:
            if name not in done:
                emit_piece(name)
        emit(item)
        while True:
            ready = [p for p in pending if all(n in done for n in p.needs)]
            urgent = [p for p in ready if p.urgent]
            behind = spent["valu"] * total_mxu < spent["mxu"] * total_valu
            if urgent:
                emit_piece(urgent[0].name)
            elif ready and behind:
                emit_piece(ready[0].name)
            else:
                break
    for piece in list(pending):
        emit_piece(piece.name)


def _layer_kernel(*refs, tm, d, dff, tiles_per_seq, n_tiles, final_norm):
    x_ref, xn_ref, mixer_w, ffn_w, o_ref = refs[0], refs[1], refs[2:13], refs[13:19], refs[19]
    n_mixer_scr = 2 * (d // MXU_COLS) + 8
    mixer_scr, x1_scr, ffn_scr = refs[20:20 + n_mixer_scr], refs[20 + n_mixer_scr], refs[21 + n_mixer_scr:]
    i = pl.program_id(0)
    jm = lax.rem(jnp.minimum(i, n_tiles - 1), tiles_per_seq)
    mixer = _MixerStage(x_ref.at[0], x1_scr, jm, mixer_w, mixer_scr, tm=tm, d=d)
    ffn = _FfnStage(x1_scr, o_ref.at[0], ffn_w, ffn_scr, tm=tm, d=d, dff=dff, final_norm=final_norm)

    @pl.when(i == 0)
    def _():
        mixer.normalize_input(x_ref.at[0])
        x1_scr[...] = jnp.zeros((tm, d), jnp.float32)
        ffn.h_scr[...] = jnp.zeros((tm, d), jnp.bfloat16)
        ffn.zero_halos()

    @pl.when(jm == 0)
    def _():
        mixer.zero_halos()

    @pl.when(lax.rem(i + tiles_per_seq - 1, tiles_per_seq) == 0)
    def _():
        ffn.zero_halos()

    ffn.begin()

    P = functools.partial
    nq, nk = mixer.n_chunks, ffn.n_chunks
    row_groups = range(0, tm, CONV_GROUP_ROWS)
    dot_cycles = tm
    conv_names = [f"conv{q}.{s}.{r0}" for q in range(nq) for s in range(STRIPS_PER_CHUNK)
                  for r0 in row_groups]
    pool_names = [f"pool{q}.{s}" for q in range(nq) for s in range(STRIPS_PER_CHUNK)]
    norm_names = [f"norm{r0}" for r0 in row_groups]

    def conv_in(q):
        return _Item(f"PV{q}", 2 * dot_cycles, P(mixer.project_conv_in, q), valu=tm // 2)

    def pool_in(q):
        return _Item(f"PP{q}", dot_cycles, P(mixer.project_pool_in, q))

    def gates(which, ns):
        return [_Item(f"G{which}.{n}", dot_cycles, P(mixer.gate_logits, which, n)) for n in ns]

    def up(k):
        return _Item(f"UP{k}", 2 * dot_cycles, P(ffn.up_project, k))

    def down(k):
        needs = [f"act{k}.{s}" for s in range(STRIPS_PER_CHUNK)]
        return _Item(f"DN{k}", dot_cycles, P(ffn.down_project, k), needs=needs, valu=tm // 4)

    plan = [conv_in(0), pool_in(0), up(0), conv_in(1), pool_in(1), up(1)]
    mixer_fill = [gates(0, (0, 1)), [conv_in(2), pool_in(2)], gates(0, (2, 3)),
                  [conv_in(3), pool_in(3)], gates(1, (0, 1)), gates(1, (2, 3))]
    for k in range(2, nk + 2):
        if mixer_fill:
            plan += mixer_fill.pop(0)
        if k < nk:
            plan.append(up(k))
        plan.append(down(k - 2))
    assert not mixer_fill
    plan += [_Item(f"M{n}", dot_cycles * 5 // 4, P(mixer.merge_chunk, n), needs=norm_names,
                   valu=3 * tm // 4) for n in range(nq)]
    plan += [_Item(f"O{n}", dot_cycles, P(mixer.out_chunk, n), valu=tm // 8) for n in range(nq)]

    pieces = []
    for k in range(nk):
        pieces += [_Item(f"act{k}.{s}", 0, P(ffn.activate_piece, k, s), needs=[f"UP{k}"], valu=3 * tm // 4,
                         urgent=True) for s in range(STRIPS_PER_CHUNK)]
    pieces.append(_Item("ffn_out", 0, ffn.epilogue, needs=[f"DN{nk - 1}"], valu=tm))
    for q in range(nq):
        for s in range(STRIPS_PER_CHUNK):
            pieces += [_Item(f"conv{q}.{s}.{r0}", 0, P(mixer.conv_piece, q, s, r0), needs=[f"PV{q}"],
                             valu=5 * CONV_GROUP_ROWS // 2) for r0 in row_groups]
            pieces.append(_Item(f"pool{q}.{s}", 0, P(mixer.pool_piece, q, s), needs=[f"PP{q}"], valu=tm // 4))
    pieces.append(_Item("carry", 0, mixer.carry_halos, needs=conv_names + pool_names, valu=tm // 8))
    pieces += [_Item(f"norm{r0}", 0, P(mixer.normalize_piece, r0), needs=conv_names + pool_names,
                     valu=7 * CONV_GROUP_ROWS // 2) for r0 in row_groups]
    _emit_interleaved(plan, pieces)

    mixer.normalize_input(xn_ref.at[0])
    ffn.normalize_input()


class _Layer:
    def __init__(self, stacked, layer):
        self.stacked, self.layer = stacked, layer


def _resident(arg):
    if isinstance(arg, _Layer):
        tail = arg.stacked.shape[1:]
        index = (arg.layer,) + (0,) * len(tail)
        return pl.BlockSpec((None,) + tail, lambda i: index, pipeline_mode=pl.Buffered(1))
    zeros = (0,) * arg.ndim
    return pl.BlockSpec(arg.shape, lambda i: zeros, pipeline_mode=pl.Buffered(1))


def _operand(arg):
    return arg.stacked if isinstance(arg, _Layer) else arg


def _layer(x, mixer_args, ffn_args, *, tm, final_norm):
    bsz, seq, d = x.shape
    dff = 2 * ffn_args[4].stacked.shape[1]
    tiles_per_seq = seq // tm
    n_tiles = bsz * tiles_per_seq

    def tile_index(t):
        return (t // tiles_per_seq, t % tiles_per_seq, 0)

    x_spec = pl.BlockSpec((1, tm, d), lambda i: tile_index(jnp.minimum(i, n_tiles - 1)))
    xn_spec = pl.BlockSpec((1, tm, d), lambda i: tile_index(jnp.minimum(i + 1, n_tiles - 1)))
    o_spec = pl.BlockSpec((1, tm, d), lambda i: tile_index(jnp.maximum(i - 1, 0)))
    args = tuple(mixer_args) + tuple(ffn_args)
    f32_tile = pltpu.VMEM((tm, d), jnp.float32)
    bf16_tile = pltpu.VMEM((tm, d), jnp.bfloat16)
    return pl.pallas_call(
        functools.partial(_layer_kernel, tm=tm, d=d, dff=dff, tiles_per_seq=tiles_per_seq,
                          n_tiles=n_tiles, final_norm=final_norm),
        grid=(n_tiles + 1,),
        in_specs=[x_spec, xn_spec] + [_resident(a) for a in args],
        out_specs=o_spec,
        out_shape=jax.ShapeDtypeStruct(x.shape, x.dtype),
        scratch_shapes=(
            [pltpu.VMEM((STRIPS_PER_CHUNK, CONV_HALO + tm, LANES), jnp.float32)] * (d // MXU_COLS)
            + [pltpu.VMEM((STRIPS_PER_CHUNK, POOL_HALO + tm, LANES), jnp.float32)] * (d // MXU_COLS)
            + [bf16_tile,
               f32_tile, f32_tile, f32_tile, f32_tile,
               bf16_tile, bf16_tile, bf16_tile,
               f32_tile]
            + [pltpu.VMEM((2 * STRIPS_PER_CHUNK, FFN_HALO + tm, LANES), jnp.float32)] * (dff // MXU_COLS)
            + [bf16_tile,
               pltpu.VMEM((tm, dff), jnp.bfloat16)]),
        compiler_params=pltpu.CompilerParams(
            dimension_semantics=("arbitrary",),
            vmem_limit_bytes=VMEM_LIMIT_BYTES),
        name="layer",
    )(x, x, *[_operand(a) for a in args])


def kernel(x, w_in, b_in, conv_dw_w, conv_dw_b, conv_ln_g, conv_ln_b, conv_w_out, pool_w, pool_scale, mix_w_out, norm_mix, ffn_w_up, ffn_dw_w, ffn_dw_b, ffn_w_down, norm_ffn, norm_final):
    depth = w_in.shape[0]
    row = lambda v: v.reshape(1, -1)
    tm = TIME_TILE
    win, cwo, mwo = _pack_bf16(w_in), _pack_bf16(conv_w_out), _pack_bf16(mix_w_out)
    pw = _pack_bf16(pool_w.reshape(depth, -1, pool_w.shape[-1]))
    wup, wdn = _pack_bf16(ffn_w_up), _pack_bf16(ffn_w_down)
    for l in range(depth):
        mixer_args = (row(norm_mix[l]), _Layer(win, l), row(b_in[l]), conv_dw_w[l],
                      row(conv_dw_b[l]), row(conv_ln_g[l]), row(conv_ln_b[l]), _Layer(cwo, l),
                      _Layer(pw, l), row(pool_scale[l]), _Layer(mwo, l))
        ffn_args = (row(norm_ffn[l]), _Layer(wup, l), ffn_dw_w[l], row(ffn_dw_b[l]),
                    _Layer(wdn, l), row(norm_final))
        x = _layer(x, mixer_args, ffn_args, tm=tm, final_norm=(l == depth - 1))
    return x
```

```python
import functools
import math

import jax
import jax.numpy as jnp
from jax import lax
from jax.experimental import pallas as pl
from jax.experimental.pallas import tpu as pltpu

LANES = 128
SUBLANES = 8
MXU_COLS = 256
VMEM_LIMIT_BYTES = 60 * 1024 * 1024

EPS = 1e-6
CONV_KERNEL = 31
POOL_WINDOWS = (2, 4, 8, 16)
FFN_KERNEL = 3

CONV_HALO = 32
POOL_HALO = 16
FFN_HALO = 8
PACKED_ROWS = 16
CONV_GROUP_ROWS = 64
PACK_BLOCK_BYTES = 6 * 1024 * 1024
TIME_TILE = 256
STRIPS_PER_CHUNK = MXU_COLS // LANES

GELU_C = math.sqrt(2.0 / math.pi)


def _rmsnorm(x, g):
    ms = jnp.mean(x * x, axis=-1, keepdims=True)
    return x * lax.rsqrt(ms + EPS) * g


def _sigmoid(x):
    return 0.5 + 0.5 * jnp.tanh(0.5 * x)


def _gelu_tanh(x):
    hx = 0.5 * x
    return hx + hx * jnp.tanh(x * (GELU_C + (0.044715 * GELU_C) * (x * x)))


def _dot(a, b):
    return jnp.dot(a, b, preferred_element_type=jnp.float32)


def _pack_kernel(w_ref, o_ref):
    o_ref[...] = pltpu.bitcast(w_ref[...].astype(jnp.bfloat16), jnp.int32)


def _pack_bf16(w):
    n_layers, k, n = w.shape
    bk = max(b for b in range(PACKED_ROWS, k + 1, PACKED_ROWS)
             if k % b == 0 and b * n * 4 <= PACK_BLOCK_BYTES)
    return pl.pallas_call(
        _pack_kernel,
        grid=(n_layers, k // bk),
        in_specs=[pl.BlockSpec((None, bk, n), lambda l, i: (l, i, 0))],
        out_specs=pl.BlockSpec((None, bk // 2, n), lambda l, i: (l, i, 0)),
        out_shape=jax.ShapeDtypeStruct((n_layers, k // 2, n), jnp.int32),
        compiler_params=pltpu.CompilerParams(
            dimension_semantics=("arbitrary", "arbitrary"),
            vmem_limit_bytes=VMEM_LIMIT_BYTES),
        name="pack_bf16",
    )(w)


def _unpack(packed):
    return pltpu.bitcast(packed, jnp.bfloat16)


def _vreg_row(ref, k, cs):
    return jnp.broadcast_to(ref[k:k + 1, cs], (SUBLANES, LANES))


class _MixerStage:
    def __init__(self, x_tile, out_tile, j, weights, scratch, *, tm, d):
        (self.nrm_ref, self.win_ref, self.bin_ref, self.dww_ref, self.dwb_ref, self.lng_ref,
         self.lnb_ref, self.cwo_ref, self.pw_ref, self.ps_ref, self.mwo_ref) = weights
        self.n_chunks = d // MXU_COLS
        assert self.n_chunks == len(POOL_WINDOWS)
        self.u_ext, scratch = scratch[:self.n_chunks], scratch[self.n_chunks:]
        self.p_ext, scratch = scratch[:self.n_chunks], scratch[self.n_chunks:]
        (self.h_scr, self.conv_scr, self.pooled_scr, self.ga_scr,
         self.gb_scr, self.act_scr, self.pooled_bf, self.merged_scr) = scratch
        self.x_tile, self.out_tile, self.j, self.tm, self.d = x_tile, out_tile, j, tm, d

    def zero_halos(self):
        for q in range(self.n_chunks):
            self.u_ext[q][:, 0:CONV_HALO, :] = jnp.zeros((STRIPS_PER_CHUNK, CONV_HALO, LANES), jnp.float32)
            self.p_ext[q][:, 0:POOL_HALO, :] = jnp.zeros((STRIPS_PER_CHUNK, POOL_HALO, LANES), jnp.float32)

    def normalize_input(self, x_tile):
        self.h_scr[...] = _rmsnorm(x_tile[...], self.nrm_ref[...]).astype(jnp.bfloat16)

    def _proj(self, col0, cols):
        sl = slice(col0 + cols.start, col0 + cols.stop)
        return _dot(self.h_scr[...], _unpack(self.win_ref[:, sl])) + self.bin_ref[:, sl]

    def project_conv_in(self, q):
        tm, d = self.tm, self.d
        cols = slice(q * MXU_COLS, (q + 1) * MXU_COLS)
        u = self._proj(0, cols) * _sigmoid(self._proj(d, cols))
        for s in range(STRIPS_PER_CHUNK):
            self.u_ext[q][s, CONV_HALO:CONV_HALO + tm, :] = u[:, s * LANES:(s + 1) * LANES]

    def project_pool_in(self, q):
        tm, d = self.tm, self.d
        pin = self._proj(2 * d, slice(q * MXU_COLS, (q + 1) * MXU_COLS))
        for s in range(STRIPS_PER_CHUNK):
            self.p_ext[q][s, POOL_HALO:POOL_HALO + tm, :] = pin[:, s * LANES:(s + 1) * LANES]

    def conv_piece(self, q, s, r0):
        conv_base = CONV_HALO - (CONV_KERNEL - 1)
        c = q * STRIPS_PER_CHUNK + s
        cs = slice(c * LANES, (c + 1) * LANES)
        ws = [_vreg_row(self.dww_ref, k, cs) for k in range(CONV_KERNEL)]
        blocks = range(r0, r0 + CONV_GROUP_ROWS, SUBLANES)
        accs = {r: _vreg_row(self.dwb_ref, 0, cs) for r in blocks}
        for lo in range(r0 + conv_base, r0 + conv_base + CONV_GROUP_ROWS + CONV_KERNEL - 1):
            taps = [(r, lo - conv_base - r) for r in blocks
                    if 0 <= lo - conv_base - r < CONV_KERNEL]
            win = self.u_ext[q][s, lo:lo + SUBLANES, :]
            for r, k in taps:
                accs[r] = accs[r] + ws[k] * win
        for r in blocks:
            self.conv_scr[r:r + SUBLANES, cs] = accs[r]

    def pool_piece(self, q, s):
        tm = self.tm
        row_iota = lax.broadcasted_iota(jnp.int32, (SUBLANES, LANES), 0)
        window = POOL_WINDOWS[q]
        c = q * STRIPS_PER_CHUNK + s
        cs = slice(c * LANES, (c + 1) * LANES)
        for r in range(0, tm, SUBLANES):
            tok = self.p_ext[q][s, r + POOL_HALO:r + POOL_HALO + SUBLANES, :]
            acc = tok
            for back in range(1, window):
                lo = r + POOL_HALO - back
                acc = acc + self.p_ext[q][s, lo:lo + SUBLANES, :]
            pos1 = (self.j * tm + (r + 1) + row_iota).astype(jnp.float32)
            cnt = jnp.minimum(pos1, float(window))
            self.pooled_scr[r:r + SUBLANES, cs] = acc / cnt - tok

    def gate_logits(self, which, n):
        col0, dst = ((3 * self.d, self.ga_scr), (4 * self.d, self.gb_scr))[which]
        cols = slice(n * MXU_COLS, (n + 1) * MXU_COLS)
        dst[:, cols] = _dot(self.h_scr[...],
                            _unpack(self.win_ref[:, col0 + cols.start:col0 + cols.stop]))

    def _gate(self, which, cols):
        col0, src = ((3 * self.d, self.ga_scr), (4 * self.d, self.gb_scr))[which]
        return _sigmoid(src[:, cols] + self.bin_ref[:, col0 + cols.start:col0 + cols.stop])

    def carry_halos(self):
        tm = self.tm
        for q in range(self.n_chunks):
            self.u_ext[q][:, 0:CONV_HALO, :] = self.u_ext[q][:, tm:tm + CONV_HALO, :]
            self.p_ext[q][:, 0:POOL_HALO, :] = self.p_ext[q][:, tm:tm + POOL_HALO, :]

    def normalize_piece(self, r0):
        rows = slice(r0, r0 + CONV_GROUP_ROWS)
        y = self.conv_scr[rows, :]
        mu = jnp.mean(y, axis=-1, keepdims=True)
        yc = y - mu
        var = jnp.mean(yc * yc, axis=-1, keepdims=True)
        yn = yc * lax.rsqrt(var + EPS) * self.lng_ref[...] + self.lnb_ref[...]
        self.act_scr[rows, :] = (yn * _sigmoid(yn)).astype(jnp.bfloat16)
        self.pooled_bf[rows, :] = self.pooled_scr[rows, :].astype(jnp.bfloat16)

    def merge_chunk(self, n):
        cols = slice(n * MXU_COLS, (n + 1) * MXU_COLS)
        ya = _dot(self.act_scr[...], _unpack(self.cwo_ref[:, cols]))
        group_rows = slice(cols.start // 2, cols.stop // 2)
        yb = _dot(self.pooled_bf[:, cols], _unpack(self.pw_ref[group_rows, :])) * self.ps_ref[:, cols]
        merged = self._gate(0, cols) * ya + self._gate(1, cols) * yb
        self.merged_scr[:, cols] = merged.astype(jnp.bfloat16)

    def out_chunk(self, n):
        cols = slice(n * MXU_COLS, (n + 1) * MXU_COLS)
        self.out_tile[:, cols] = self.x_tile[:, cols] + _dot(
            self.merged_scr[...], _unpack(self.mwo_ref[:, cols]))


class _FfnStage:
    def __init__(self, x_tile, out_tile, weights, scratch, *, tm, d, dff, final_norm):
        self.nrm_ref, self.wup_ref, self.dww_ref, self.dwb_ref, self.wdn_ref, self.nf_ref = weights
        self.n_chunks = dff // MXU_COLS
        self.u_ext, (self.h_scr, self.act_scr) = scratch[:self.n_chunks], scratch[self.n_chunks:]
        self.x_tile, self.out_tile = x_tile, out_tile
        self.tm, self.d, self.dff, self.final_norm = tm, d, dff, final_norm
        self.y = None

    def zero_halos(self):
        for q in range(self.n_chunks):
            self.u_ext[q][:, 0:FFN_HALO, :] = jnp.zeros((2 * STRIPS_PER_CHUNK, FFN_HALO, LANES), jnp.float32)

    def normalize_input(self):
        self.h_scr[...] = _rmsnorm(self.x_tile[...], self.nrm_ref[...]).astype(jnp.bfloat16)

    def begin(self):
        self.y = self.x_tile[...]

    def _conv3(self, q, local, r):
        base = FFN_HALO - (FFN_KERNEL - 1)
        is_gate, s = divmod(local, STRIPS_PER_CHUNK)
        col0 = is_gate * self.dff + (q * STRIPS_PER_CHUNK + s) * LANES
        cs = slice(col0, col0 + LANES)
        strip = self.u_ext[q]
        acc = (_vreg_row(self.dww_ref, 0, cs) * strip[local, r + base:r + base + SUBLANES, :]
               + _vreg_row(self.dwb_ref, 0, cs))
        for k in range(1, FFN_KERNEL):
            lo = r + base + k
            acc = acc + _vreg_row(self.dww_ref, k, cs) * strip[local, lo:lo + SUBLANES, :]
        return acc

    def up_project(self, q):
        tm, dff = self.tm, self.dff
        cols = slice(q * MXU_COLS, (q + 1) * MXU_COLS)
        a = _dot(self.h_scr[...], _unpack(self.wup_ref[:, cols]))
        g = _dot(self.h_scr[...], _unpack(self.wup_ref[:, dff + cols.start:dff + cols.stop]))
        for s in range(STRIPS_PER_CHUNK):
            ls = slice(s * LANES, (s + 1) * LANES)
            self.u_ext[q][s, FFN_HALO:FFN_HALO + tm, :] = a[:, ls]
            self.u_ext[q][STRIPS_PER_CHUNK + s, FFN_HALO:FFN_HALO + tm, :] = g[:, ls]

    def activate_piece(self, q, s):
        tm = self.tm
        c = q * STRIPS_PER_CHUNK + s
        for r in range(0, tm, PACKED_ROWS):
            rows = [_gelu_tanh(self._conv3(q, s, r + rr))
                    * self._conv3(q, STRIPS_PER_CHUNK + s, r + rr)
                    for rr in range(0, PACKED_ROWS, SUBLANES)]
            self.act_scr[r:r + PACKED_ROWS, c * LANES:(c + 1) * LANES] = (
                jnp.concatenate(rows, axis=0).astype(jnp.bfloat16))
        for local in (s, STRIPS_PER_CHUNK + s):
            self.u_ext[q][local, 0:FFN_HALO, :] = self.u_ext[q][local, tm:tm + FFN_HALO, :]

    def down_project(self, q):
        cols = slice(q * MXU_COLS, (q + 1) * MXU_COLS)
        packed_rows = slice(cols.start // 2, cols.stop // 2)
        self.y = self.y + _dot(self.act_scr[:, cols], _unpack(self.wdn_ref[packed_rows, :]))

    def epilogue(self):
        y = self.y
        if self.final_norm:
            y = _rmsnorm(y, self.nf_ref[...])
        self.out_tile[...] = y


class _Item:
    def __init__(self, name, mxu, emit, needs=(), valu=0, urgent=False):
        self.name, self.mxu, self.emit, self.needs, self.valu = name, mxu, emit, tuple(needs), valu
        self.urgent = urgent


def _emit_interleaved(plan, pieces):
    total_mxu = sum(it.mxu for it in plan)
    total_valu = sum(it.valu for it in plan) + sum(it.valu for it in pieces)
    done, pending = set(), list(pieces)
    spent = {"mxu": 0, "valu": 0}

    def emit(item):
        assert all(n in done for n in item.needs), (item.name, item.needs)
        item.emit()
        done.add(item.name)
        spent["mxu"] += item.mxu
        spent["valu"] += item.valu

    def emit_piece(name):
        piece = next(p for p in pending if p.name == name)
        pending.remove(piece)
        for need in piece.needs:
            if need not in done:
                emit_piece(need)
        emit(piece)

    for item in plan:
        for name in item.needs:
            if name not in done:
                emit_piece(name)
        emit(item)
        while True:
            ready = [p for p in pending if all(n in done for n in p.needs)]
            urgent = [p for p in ready if p.urgent]
            behind = spent["valu"] * total_mxu < spent["mxu"] * total_valu
            if urgent:
                emit_piece(urgent[0].name)
            elif ready and behind:
                emit_piece(ready[0].name)
            else:
                break
    for piece in list(pending):
        emit_piece(piece.name)


def _layer_kernel(*refs, tm, d, dff, tiles_per_seq, n_tiles, final_norm):
    x_ref, mixer_w, ffn_w, o_ref = refs[0], refs[1:12], refs[12:18], refs[18]
    n_mixer_scr = 2 * (d // MXU_COLS) + 8
    mixer_scr, x1_scr, ffn_scr = refs[19:19 + n_mixer_scr], refs[19 + n_mixer_scr], refs[20 + n_mixer_scr:]
    i = pl.program_id(0)
    jm = lax.rem(jnp.minimum(i, n_tiles - 1), tiles_per_seq)
    mixer = _MixerStage(x_ref.at[0], x1_scr, jm, mixer_w, mixer_scr, tm=tm, d=d)
    ffn = _FfnStage(x1_scr, o_ref.at[0], ffn_w, ffn_scr, tm=tm, d=d, dff=dff, final_norm=final_norm)

    @pl.when(i == 0)
    def _():
        x1_scr[...] = jnp.zeros((tm, d), jnp.float32)
        ffn.zero_halos()

    @pl.when(jm == 0)
    def _():
        mixer.zero_halos()

    @pl.when(lax.rem(i + tiles_per_seq - 1, tiles_per_seq) == 0)
    def _():
        ffn.zero_halos()

    ffn.begin()
    ffn.normalize_input()
    mixer.normalize_input(x_ref.at[0])

    P = functools.partial
    nq, nk = mixer.n_chunks, ffn.n_chunks
    row_groups = range(0, tm, CONV_GROUP_ROWS)
    dot_cycles = tm
    conv_names = [f"conv{q}.{s}.{r0}" for q in range(nq) for s in range(STRIPS_PER_CHUNK)
                  for r0 in row_groups]
    pool_names = [f"pool{q}.{s}" for q in range(nq) for s in range(STRIPS_PER_CHUNK)]
    norm_names = [f"norm{r0}" for r0 in row_groups]

    def conv_in(q):
        return _Item(f"PV{q}", 2 * dot_cycles, P(mixer.project_conv_in, q), valu=tm // 2)

    def pool_in(q):
        return _Item(f"PP{q}", dot_cycles, P(mixer.project_pool_in, q))

    def gates(which, ns):
        return [_Item(f"G{which}.{n}", dot_cycles, P(mixer.gate_logits, which, n)) for n in ns]

    def up(k):
        return _Item(f"UP{k}", 2 * dot_cycles, P(ffn.up_project, k))

    def down(k):
        needs = [f"act{k}.{s}" for s in range(STRIPS_PER_CHUNK)]
        return _Item(f"DN{k}", dot_cycles, P(ffn.down_project, k), needs=needs, valu=tm // 4)

    plan = [conv_in(0), pool_in(0), up(0), conv_in(1), pool_in(1), up(1)]
    mixer_fill = [gates(0, (0, 1)), [conv_in(2), pool_in(2)], gates(0, (2, 3)),
                  [conv_in(3), pool_in(3)], gates(1, (0, 1)), gates(1, (2, 3))]
    for k in range(2, nk + 2):
        if mixer_fill:
            plan += mixer_fill.pop(0)
        if k < nk:
            plan.append(up(k))
        plan.append(down(k - 2))
    assert not mixer_fill
    plan += [_Item(f"M{n}", dot_cycles * 5 // 4, P(mixer.merge_chunk, n), needs=norm_names,
                   valu=3 * tm // 4) for n in range(nq)]
    plan += [_Item(f"O{n}", dot_cycles, P(mixer.out_chunk, n), valu=tm // 8) for n in range(nq)]

    pieces = []
    for k in range(nk):
        pieces += [_Item(f"act{k}.{s}", 0, P(ffn.activate_piece, k, s), needs=[f"UP{k}"], valu=3 * tm // 4,
                         urgent=True) for s in range(STRIPS_PER_CHUNK)]
    pieces.append(_Item("ffn_out", 0, ffn.epilogue, needs=[f"DN{nk - 1}"], valu=tm))
    for q in range(nq):
        for s in range(STRIPS_PER_CHUNK):
            pieces += [_Item(f"conv{q}.{s}.{r0}", 0, P(mixer.conv_piece, q, s, r0), needs=[f"PV{q}"],
                             valu=5 * CONV_GROUP_ROWS // 2) for r0 in row_groups]
            pieces.append(_Item(f"pool{q}.{s}", 0, P(mixer.pool_piece, q, s), needs=[f"PP{q}"], valu=tm // 4))
    pieces.append(_Item("carry", 0, mixer.carry_halos, needs=conv_names + pool_names, valu=tm // 8))
    pieces += [_Item(f"norm{r0}", 0, P(mixer.normalize_piece, r0), needs=conv_names + pool_names,
                     valu=7 * CONV_GROUP_ROWS // 2) for r0 in row_groups]
    _emit_interleaved(plan, pieces)


class _Layer:
    def __init__(self, stacked, layer):
        self.stacked, self.layer = stacked, layer


def _resident(arg):
    if isinstance(arg, _Layer):
        tail = arg.stacked.shape[1:]
        index = (arg.layer,) + (0,) * len(tail)
        return pl.BlockSpec((None,) + tail, lambda i: index, pipeline_mode=pl.Buffered(1))
    zeros = (0,) * arg.ndim
    return pl.BlockSpec(arg.shape, lambda i: zeros, pipeline_mode=pl.Buffered(1))


def _operand(arg):
    return arg.stacked if isinstance(arg, _Layer) else arg


def _layer(x, mixer_args, ffn_args, *, tm, final_norm):
    bsz, seq, d = x.shape
    dff = 2 * ffn_args[4].stacked.shape[1]
    tiles_per_seq = seq // tm
    n_tiles = bsz * tiles_per_seq

    def tile_index(t):
        return (t // tiles_per_seq, t % tiles_per_seq, 0)

    x_spec = pl.BlockSpec((1, tm, d), lambda i: tile_index(jnp.minimum(i, n_tiles - 1)))
    o_spec = pl.BlockSpec((1, tm, d), lambda i: tile_index(jnp.maximum(i - 1, 0)))
    args = tuple(mixer_args) + tuple(ffn_args)
    f32_tile = pltpu.VMEM((tm, d), jnp.float32)
    bf16_tile = pltpu.VMEM((tm, d), jnp.bfloat16)
    return pl.pallas_call(
        functools.partial(_layer_kernel, tm=tm, d=d, dff=dff, tiles_per_seq=tiles_per_seq,
                          n_tiles=n_tiles, final_norm=final_norm),
        grid=(n_tiles + 1,),
        in_specs=[x_spec] + [_resident(a) for a in args],
        out_specs=o_spec,
        out_shape=jax.ShapeDtypeStruct(x.shape, x.dtype),
        scratch_shapes=(
            [pltpu.VMEM((STRIPS_PER_CHUNK, CONV_HALO + tm, LANES), jnp.float32)] * (d // MXU_COLS)
            + [pltpu.VMEM((STRIPS_PER_CHUNK, POOL_HALO + tm, LANES), jnp.float32)] * (d // MXU_COLS)
            + [bf16_tile,
               f32_tile, f32_tile, f32_tile, f32_tile,
               bf16_tile, bf16_tile, bf16_tile,
               f32_tile]
            + [pltpu.VMEM((2 * STRIPS_PER_CHUNK, FFN_HALO + tm, LANES), jnp.float32)] * (dff // MXU_COLS)
            + [bf16_tile,
               pltpu.VMEM((tm, dff), jnp.bfloat16)]),
        compiler_params=pltpu.CompilerParams(
            dimension_semantics=("arbitrary",),
            vmem_limit_bytes=VMEM_LIMIT_BYTES),
        name="layer",
    )(x, *[_operand(a) for a in args])


def kernel(x, w_in, b_in, conv_dw_w, conv_dw_b, conv_ln_g, conv_ln_b, conv_w_out, pool_w, pool_scale, mix_w_out, norm_mix, ffn_w_up, ffn_dw_w, ffn_dw_b, ffn_w_down, norm_ffn, norm_final):
    depth = w_in.shape[0]
    row = lambda v: v.reshape(1, -1)
    tm = TIME_TILE
    win, cwo, mwo = _pack_bf16(w_in), _pack_bf16(conv_w_out), _pack_bf16(mix_w_out)
    pw = _pack_bf16(pool_w.reshape(depth, -1, pool_w.shape[-1]))
    wup, wdn = _pack_bf16(ffn_w_up), _pack_bf16(ffn_w_down)
    for l in range(depth):
        mixer_args = (row(norm_mix[l]), _Layer(win, l), row(b_in[l]), conv_dw_w[l],
                      row(conv_dw_b[l]), row(conv_ln_g[l]), row(conv_ln_b[l]), _Layer(cwo, l),
                      _Layer(pw, l), row(pool_scale[l]), _Layer(mwo, l))
        ffn_args = (row(norm_ffn[l]), _Layer(wup, l), ffn_dw_w[l], row(ffn_dw_b[l]),
                    _Layer(wdn, l), row(norm_final))
        x = _layer(x, mixer_args, ffn_args, tm=tm, final_norm=(l == depth - 1))
    return x
```

```python
import functools
import math

import jax
import jax.numpy as jnp
from jax import lax
from jax.experimental import pallas as pl
from jax.experimental.pallas import tpu as pltpu

LANES = 128
SUBLANES = 8
MXU_COLS = 256
VMEM_LIMIT_BYTES = 60 * 1024 * 1024

EPS = 1e-6
CONV_KERNEL = 31
POOL_WINDOWS = (2, 4, 8, 16)
FFN_KERNEL = 3

CONV_HALO = 32
POOL_HALO = 16
FFN_HALO = 8
PACKED_ROWS = 16
CONV_GROUP_ROWS = 64
PACK_BLOCK_BYTES = 6 * 1024 * 1024
TIME_TILE = 256
IO_TILES = 2
STRIPS_PER_CHUNK = MXU_COLS // LANES

GELU_C = math.sqrt(2.0 / math.pi)


def _rmsnorm(x, g):
    ms = jnp.mean(x * x, axis=-1, keepdims=True)
    return x * lax.rsqrt(ms + EPS) * g


def _sigmoid(x):
    return 0.5 + 0.5 * jnp.tanh(0.5 * x)


def _gelu_tanh(x):
    hx = 0.5 * x
    return hx + hx * jnp.tanh(x * (GELU_C + (0.044715 * GELU_C) * (x * x)))


def _dot(a, b):
    return jnp.dot(a, b, preferred_element_type=jnp.float32)


def _pack_kernel(w_ref, o_ref):
    o_ref[...] = pltpu.bitcast(w_ref[...].astype(jnp.bfloat16), jnp.int32)


def _pack_bf16(w):
    n_layers, k, n = w.shape
    bk = max(b for b in range(PACKED_ROWS, k + 1, PACKED_ROWS)
             if k % b == 0 and b * n * 4 <= PACK_BLOCK_BYTES)
    return pl.pallas_call(
        _pack_kernel,
        grid=(n_layers, k // bk),
        in_specs=[pl.BlockSpec((None, bk, n), lambda l, i: (l, i, 0))],
        out_specs=pl.BlockSpec((None, bk // 2, n), lambda l, i: (l, i, 0)),
        out_shape=jax.ShapeDtypeStruct((n_layers, k // 2, n), jnp.int32),
        compiler_params=pltpu.CompilerParams(
            dimension_semantics=("arbitrary", "arbitrary"),
            vmem_limit_bytes=VMEM_LIMIT_BYTES),
        name="pack_bf16",
    )(w)


def _unpack(packed):
    return pltpu.bitcast(packed, jnp.bfloat16)


def _vreg_row(ref, k, cs):
    return jnp.broadcast_to(ref[k:k + 1, cs], (SUBLANES, LANES))


class _MixerStage:
    def __init__(self, x_tile, out_tile, j, weights, scratch, *, tm, d):
        (self.nrm_ref, self.win_ref, self.bin_ref, self.dww_ref, self.dwb_ref, self.lng_ref,
         self.lnb_ref, self.cwo_ref, self.pw_ref, self.ps_ref, self.mwo_ref) = weights
        self.n_chunks = d // MXU_COLS
        assert self.n_chunks == len(POOL_WINDOWS)
        self.u_ext, scratch = scratch[:self.n_chunks], scratch[self.n_chunks:]
        self.p_ext, scratch = scratch[:self.n_chunks], scratch[self.n_chunks:]
        (self.h_scr, self.conv_scr, self.pooled_scr, self.ga_scr,
         self.gb_scr, self.act_scr, self.pooled_bf, self.merged_scr) = scratch
        self.x_tile, self.out_tile, self.j, self.tm, self.d = x_tile, out_tile, j, tm, d

    def zero_halos(self):
        for q in range(self.n_chunks):
            self.u_ext[q][:, 0:CONV_HALO, :] = jnp.zeros((STRIPS_PER_CHUNK, CONV_HALO, LANES), jnp.float32)
            self.p_ext[q][:, 0:POOL_HALO, :] = jnp.zeros((STRIPS_PER_CHUNK, POOL_HALO, LANES), jnp.float32)

    def normalize_input(self, x_tile):
        self.h_scr[...] = _rmsnorm(x_tile[...], self.nrm_ref[...]).astype(jnp.bfloat16)

    def _proj(self, col0, cols):
        sl = slice(col0 + cols.start, col0 + cols.stop)
        return _dot(self.h_scr[...], _unpack(self.win_ref[:, sl])) + self.bin_ref[:, sl]

    def project_conv_in(self, q):
        tm, d = self.tm, self.d
        cols = slice(q * MXU_COLS, (q + 1) * MXU_COLS)
        u = self._proj(0, cols) * _sigmoid(self._proj(d, cols))
        for s in range(STRIPS_PER_CHUNK):
            self.u_ext[q][s, CONV_HALO:CONV_HALO + tm, :] = u[:, s * LANES:(s + 1) * LANES]

    def project_pool_in(self, q):
        tm, d = self.tm, self.d
        pin = self._proj(2 * d, slice(q * MXU_COLS, (q + 1) * MXU_COLS))
        for s in range(STRIPS_PER_CHUNK):
            self.p_ext[q][s, POOL_HALO:POOL_HALO + tm, :] = pin[:, s * LANES:(s + 1) * LANES]

    def conv_piece(self, q, s, r0):
        conv_base = CONV_HALO - (CONV_KERNEL - 1)
        c = q * STRIPS_PER_CHUNK + s
        cs = slice(c * LANES, (c + 1) * LANES)
        ws = [_vreg_row(self.dww_ref, k, cs) for k in range(CONV_KERNEL)]
        blocks = range(r0, r0 + CONV_GROUP_ROWS, SUBLANES)
        accs = {r: _vreg_row(self.dwb_ref, 0, cs) for r in blocks}
        for lo in range(r0 + conv_base, r0 + conv_base + CONV_GROUP_ROWS + CONV_KERNEL - 1):
            taps = [(r, lo - conv_base - r) for r in blocks
                    if 0 <= lo - conv_base - r < CONV_KERNEL]
            win = self.u_ext[q][s, lo:lo + SUBLANES, :]
            for r, k in taps:
                accs[r] = accs[r] + ws[k] * win
        for r in blocks:
            self.conv_scr[r:r + SUBLANES, cs] = accs[r]

    def pool_piece(self, q, s):
        tm = self.tm
        row_iota = lax.broadcasted_iota(jnp.int32, (SUBLANES, LANES), 0)
        window = POOL_WINDOWS[q]
        c = q * STRIPS_PER_CHUNK + s
        cs = slice(c * LANES, (c + 1) * LANES)
        for r in range(0, tm, SUBLANES):
            tok = self.p_ext[q][s, r + POOL_HALO:r + POOL_HALO + SUBLANES, :]
            acc = tok
            for back in range(1, window):
                lo = r + POOL_HALO - back
                acc = acc + self.p_ext[q][s, lo:lo + SUBLANES, :]
            pos1 = (self.j * tm + (r + 1) + row_iota).astype(jnp.float32)
            cnt = jnp.minimum(pos1, float(window))
            self.pooled_scr[r:r + SUBLANES, cs] = acc / cnt - tok

    def gate_logits(self, which, n):
        col0, dst = ((3 * self.d, self.ga_scr), (4 * self.d, self.gb_scr))[which]
        cols = slice(n * MXU_COLS, (n + 1) * MXU_COLS)
        dst[:, cols] = _dot(self.h_scr[...],
                            _unpack(self.win_ref[:, col0 + cols.start:col0 + cols.stop]))

    def _gate(self, which, cols):
        col0, src = ((3 * self.d, self.ga_scr), (4 * self.d, self.gb_scr))[which]
        return _sigmoid(src[:, cols] + self.bin_ref[:, col0 + cols.start:col0 + cols.stop])

    def carry_halos(self):
        tm = self.tm
        for q in range(self.n_chunks):
            self.u_ext[q][:, 0:CONV_HALO, :] = self.u_ext[q][:, tm:tm + CONV_HALO, :]
            self.p_ext[q][:, 0:POOL_HALO, :] = self.p_ext[q][:, tm:tm + POOL_HALO, :]

    def normalize_piece(self, r0):
        rows = slice(r0, r0 + CONV_GROUP_ROWS)
        y = self.conv_scr[rows, :]
        mu = jnp.mean(y, axis=-1, keepdims=True)
        yc = y - mu
        var = jnp.mean(yc * yc, axis=-1, keepdims=True)
        yn = yc * lax.rsqrt(var + EPS) * self.lng_ref[...] + self.lnb_ref[...]
        self.act_scr[rows, :] = (yn * _sigmoid(yn)).astype(jnp.bfloat16)
        self.pooled_bf[rows, :] = self.pooled_scr[rows, :].astype(jnp.bfloat16)

    def merge_chunk(self, n):
        cols = slice(n * MXU_COLS, (n + 1) * MXU_COLS)
        ya = _dot(self.act_scr[...], _unpack(self.cwo_ref[:, cols]))
        group_rows = slice(cols.start // 2, cols.stop // 2)
        yb = _dot(self.pooled_bf[:, cols], _unpack(self.pw_ref[group_rows, :])) * self.ps_ref[:, cols]
        merged = self._gate(0, cols) * ya + self._gate(1, cols) * yb
        self.merged_scr[:, cols] = merged.astype(jnp.bfloat16)

    def out_chunk(self, n):
        cols = slice(n * MXU_COLS, (n + 1) * MXU_COLS)
        self.out_tile[:, cols] = self.x_tile[:, cols] + _dot(
            self.merged_scr[...], _unpack(self.mwo_ref[:, cols]))


class _FfnStage:
    def __init__(self, x_tile, out_tile, weights, scratch, *, tm, d, dff, final_norm):
        self.nrm_ref, self.wup_ref, self.dww_ref, self.dwb_ref, self.wdn_ref, self.nf_ref = weights
        self.n_chunks = dff // MXU_COLS
        self.u_ext, (self.h_scr, self.act_scr) = scratch[:self.n_chunks], scratch[self.n_chunks:]
        self.x_tile, self.out_tile = x_tile, out_tile
        self.tm, self.d, self.dff, self.final_norm = tm, d, dff, final_norm
        self.y = None

    def zero_halos(self):
        for q in range(self.n_chunks):
            self.u_ext[q][:, 0:FFN_HALO, :] = jnp.zeros((2 * STRIPS_PER_CHUNK, FFN_HALO, LANES), jnp.float32)

    def normalize_input(self):
        self.h_scr[...] = _rmsnorm(self.x_tile[...], self.nrm_ref[...]).astype(jnp.bfloat16)

    def begin(self):
        self.y = self.x_tile[...]

    def _conv3(self, q, local, r):
        base = FFN_HALO - (FFN_KERNEL - 1)
        is_gate, s = divmod(local, STRIPS_PER_CHUNK)
        col0 = is_gate * self.dff + (q * STRIPS_PER_CHUNK + s) * LANES
        cs = slice(col0, col0 + LANES)
        strip = self.u_ext[q]
        acc = (_vreg_row(self.dww_ref, 0, cs) * strip[local, r + base:r + base + SUBLANES, :]
               + _vreg_row(self.dwb_ref, 0, cs))
        for k in range(1, FFN_KERNEL):
            lo = r + base + k
            acc = acc + _vreg_row(self.dww_ref, k, cs) * strip[local, lo:lo + SUBLANES, :]
        return acc

    def up_project(self, q):
        tm, dff = self.tm, self.dff
        cols = slice(q * MXU_COLS, (q + 1) * MXU_COLS)
        a = _dot(self.h_scr[...], _unpack(self.wup_ref[:, cols]))
        g = _dot(self.h_scr[...], _unpack(self.wup_ref[:, dff + cols.start:dff + cols.stop]))
        for s in range(STRIPS_PER_CHUNK):
            ls = slice(s * LANES, (s + 1) * LANES)
            self.u_ext[q][s, FFN_HALO:FFN_HALO + tm, :] = a[:, ls]
            self.u_ext[q][STRIPS_PER_CHUNK + s, FFN_HALO:FFN_HALO + tm, :] = g[:, ls]

    def activate_piece(self, q, s):
        tm = self.tm
        c = q * STRIPS_PER_CHUNK + s
        for r in range(0, tm, PACKED_ROWS):
            rows = [_gelu_tanh(self._conv3(q, s, r + rr))
                    * self._conv3(q, STRIPS_PER_CHUNK + s, r + rr)
                    for rr in range(0, PACKED_ROWS, SUBLANES)]
            self.act_scr[r:r + PACKED_ROWS, c * LANES:(c + 1) * LANES] = (
                jnp.concatenate(rows, axis=0).astype(jnp.bfloat16))
        for local in (s, STRIPS_PER_CHUNK + s):
            self.u_ext[q][local, 0:FFN_HALO, :] = self.u_ext[q][local, tm:tm + FFN_HALO, :]

    def down_project(self, q):
        cols = slice(q * MXU_COLS, (q + 1) * MXU_COLS)
        packed_rows = slice(cols.start // 2, cols.stop // 2)
        self.y = self.y + _dot(self.act_scr[:, cols], _unpack(self.wdn_ref[packed_rows, :]))

    def epilogue(self):
        y = self.y
        if self.final_norm:
            y = _rmsnorm(y, self.nf_ref[...])
        self.out_tile[...] = y


class _TileOfBlock:
    def __init__(self, ref, row0, tm):
        self.ref, self.rows = ref, pl.ds(row0, tm)

    def _cols(self, idx):
        if idx is Ellipsis:
            return slice(None)
        rows, cols = idx
        assert rows == slice(None)
        return cols

    def __getitem__(self, idx):
        return self.ref[0, self.rows, self._cols(idx)]

    def __setitem__(self, idx, value):
        self.ref[0, self.rows, self._cols(idx)] = value


class _Item:
    def __init__(self, name, mxu, emit, needs=(), valu=0, urgent=False):
        self.name, self.mxu, self.emit, self.needs, self.valu = name, mxu, emit, tuple(needs), valu
        self.urgent = urgent


def _emit_interleaved(plan, pieces):
    total_mxu = sum(it.mxu for it in plan)
    total_valu = sum(it.valu for it in plan) + sum(it.valu for it in pieces)
    done, pending = set(), list(pieces)
    spent = {"mxu": 0, "valu": 0}

    def emit(item):
        assert all(n in done for n in item.needs), (item.name, item.needs)
        item.emit()
        done.add(item.name)
        spent["mxu"] += item.mxu
        spent["valu"] += item.valu

    def emit_piece(name):
        piece = next(p for p in pending if p.name == name)
        pending.remove(piece)
        for need in piece.needs:
            if need not in done:
                emit_piece(need)
        emit(piece)

    for item in plan:
        for name in item.needs:
            if name not in done:
                emit_piece(name)
        emit(item)
        while True:
            ready = [p for p in pending if all(n in done for n in p.needs)]
            urgent = [p for p in ready if p.urgent]
            behind = spent["valu"] * total_mxu < spent["mxu"] * total_valu
            if urgent:
                emit_piece(urgent[0].name)
            elif ready and behind:
                emit_piece(ready[0].name)
            else:
                break
    for piece in list(pending):
        emit_piece(piece.name)


def _layer_kernel(*refs, tm, d, dff, tiles_per_seq, n_tiles, final_norm):
    x_ref, mixer_w, ffn_w, o_ref = refs[0], refs[1:12], refs[12:18], refs[18]
    n_mixer_scr = 2 * (d // MXU_COLS) + 8
    mixer_scr, x1_scr, ffn_scr = refs[19:19 + n_mixer_scr], refs[19 + n_mixer_scr], refs[20 + n_mixer_scr:]
    i = pl.program_id(0)
    t_in = jnp.minimum(i, n_tiles - 1)
    t_out = jnp.maximum(i - 1, 0)
    jm = lax.rem(t_in, tiles_per_seq)

    def tile_of_block(ref, t):
        return _TileOfBlock(ref, pl.multiple_of(lax.rem(t, IO_TILES) * tm, tm), tm)

    mixer = _MixerStage(tile_of_block(x_ref, t_in), x1_scr, jm, mixer_w, mixer_scr, tm=tm, d=d)
    ffn = _FfnStage(x1_scr, tile_of_block(o_ref, t_out), ffn_w, ffn_scr, tm=tm, d=d, dff=dff,
                    final_norm=final_norm)

    @pl.when(i == 0)
    def _():
        x1_scr[...] = jnp.zeros((tm, d), jnp.float32)
        ffn.zero_halos()

    @pl.when(jm == 0)
    def _():
        mixer.zero_halos()

    @pl.when(lax.rem(i + tiles_per_seq - 1, tiles_per_seq) == 0)
    def _():
        ffn.zero_halos()

    ffn.begin()
    ffn.normalize_input()
    mixer.normalize_input(mixer.x_tile)

    P = functools.partial
    nq, nk = mixer.n_chunks, ffn.n_chunks
    row_groups = range(0, tm, CONV_GROUP_ROWS)
    dot_cycles = tm
    conv_names = [f"conv{q}.{s}.{r0}" for q in range(nq) for s in range(STRIPS_PER_CHUNK)
                  for r0 in row_groups]
    pool_names = [f"pool{q}.{s}" for q in range(nq) for s in range(STRIPS_PER_CHUNK)]
    norm_names = [f"norm{r0}" for r0 in row_groups]

    def conv_in(q):
        return _Item(f"PV{q}", 2 * dot_cycles, P(mixer.project_conv_in, q), valu=tm // 2)

    def pool_in(q):
        return _Item(f"PP{q}", dot_cycles, P(mixer.project_pool_in, q))

    def gates(which, ns):
        return [_Item(f"G{which}.{n}", dot_cycles, P(mixer.gate_logits, which, n)) for n in ns]

    def up(k):
        return _Item(f"UP{k}", 2 * dot_cycles, P(ffn.up_project, k))

    def down(k):
        needs = [f"act{k}.{s}" for s in range(STRIPS_PER_CHUNK)]
        return _Item(f"DN{k}", dot_cycles, P(ffn.down_project, k), needs=needs, valu=tm // 4)

    plan = [conv_in(0), pool_in(0), up(0), conv_in(1), pool_in(1), up(1)]
    mixer_fill = [gates(0, (0, 1)), [conv_in(2), pool_in(2)], gates(0, (2, 3)),
                  [conv_in(3), pool_in(3)], gates(1, (0, 1)), gates(1, (2, 3))]
    for k in range(2, nk + 2):
        if mixer_fill:
            plan += mixer_fill.pop(0)
        if k < nk:
            plan.append(up(k))
        plan.append(down(k - 2))
    assert not mixer_fill
    plan += [_Item(f"M{n}", dot_cycles * 5 // 4, P(mixer.merge_chunk, n), needs=norm_names,
                   valu=3 * tm // 4) for n in range(nq)]
    plan += [_Item(f"O{n}", dot_cycles, P(mixer.out_chunk, n), valu=tm // 8) for n in range(nq)]

    pieces = []
    for k in range(nk):
        pieces += [_Item(f"act{k}.{s}", 0, P(ffn.activate_piece, k, s), needs=[f"UP{k}"], valu=3 * tm // 4,
                         urgent=True) for s in range(STRIPS_PER_CHUNK)]
    pieces.append(_Item("ffn_out", 0, ffn.epilogue, needs=[f"DN{nk - 1}"], valu=tm))
    for q in range(nq):
        for s in range(STRIPS_PER_CHUNK):
            pieces += [_Item(f"conv{q}.{s}.{r0}", 0, P(mixer.conv_piece, q, s, r0), needs=[f"PV{q}"],
                             valu=5 * CONV_GROUP_ROWS // 2) for r0 in row_groups]
            pieces.append(_Item(f"pool{q}.{s}", 0, P(mixer.pool_piece, q, s), needs=[f"PP{q}"], valu=tm // 4))
    pieces.append(_Item("carry", 0, mixer.carry_halos, needs=conv_names + pool_names, valu=tm // 8))
    pieces += [_Item(f"norm{r0}", 0, P(mixer.normalize_piece, r0), needs=conv_names + pool_names,
                     valu=7 * CONV_GROUP_ROWS // 2) for r0 in row_groups]
    _emit_interleaved(plan, pieces)


class _Layer:
    def __init__(self, stacked, layer):
        self.stacked, self.layer = stacked, layer


def _resident(arg):
    if isinstance(arg, _Layer):
        tail = arg.stacked.shape[1:]
        index = (arg.layer,) + (0,) * len(tail)
        return pl.BlockSpec((None,) + tail, lambda i: index, pipeline_mode=pl.Buffered(1))
    zeros = (0,) * arg.ndim
    return pl.BlockSpec(arg.shape, lambda i: zeros, pipeline_mode=pl.Buffered(1))


def _operand(arg):
    return arg.stacked if isinstance(arg, _Layer) else arg


def _layer(x, mixer_args, ffn_args, *, tm, final_norm):
    bsz, seq, d = x.shape
    dff = 2 * ffn_args[4].stacked.shape[1]
    tiles_per_seq = seq // tm
    n_tiles = bsz * tiles_per_seq

    assert tiles_per_seq % IO_TILES == 0

    def block_index(t):
        return (t // tiles_per_seq, (t % tiles_per_seq) // IO_TILES, 0)

    x_spec = pl.BlockSpec((1, IO_TILES * tm, d), lambda i: block_index(jnp.minimum(i, n_tiles - 1)))
    o_spec = pl.BlockSpec((1, IO_TILES * tm, d), lambda i: block_index(jnp.maximum(i - 1, 0)))
    args = tuple(mixer_args) + tuple(ffn_args)
    f32_tile = pltpu.VMEM((tm, d), jnp.float32)
    bf16_tile = pltpu.VMEM((tm, d), jnp.bfloat16)
    return pl.pallas_call(
        functools.partial(_layer_kernel, tm=tm, d=d, dff=dff, tiles_per_seq=tiles_per_seq,
                          n_tiles=n_tiles, final_norm=final_norm),
        grid=(n_tiles + 1,),
        in_specs=[x_spec] + [_resident(a) for a in args],
        out_specs=o_spec,
        out_shape=jax.ShapeDtypeStruct(x.shape, x.dtype),
        scratch_shapes=(
            [pltpu.VMEM((STRIPS_PER_CHUNK, CONV_HALO + tm, LANES), jnp.float32)] * (d // MXU_COLS)
            + [pltpu.VMEM((STRIPS_PER_CHUNK, POOL_HALO + tm, LANES), jnp.float32)] * (d // MXU_COLS)
            + [bf16_tile,
               f32_tile, f32_tile, f32_tile, f32_tile,
               bf16_tile, bf16_tile, bf16_tile,
               f32_tile]
            + [pltpu.VMEM((2 * STRIPS_PER_CHUNK, FFN_HALO + tm, LANES), jnp.float32)] * (dff // MXU_COLS)
            + [bf16_tile,
               pltpu.VMEM((tm, dff), jnp.bfloat16)]),
        compiler_params=pltpu.CompilerParams(
            dimension_semantics=("arbitrary",),
            vmem_limit_bytes=VMEM_LIMIT_BYTES),
        name="layer",
    )(x, *[_operand(a) for a in args])


def kernel(x, w_in, b_in, conv_dw_w, conv_dw_b, conv_ln_g, conv_ln_b, conv_w_out, pool_w, pool_scale, mix_w_out, norm_mix, ffn_w_up, ffn_dw_w, ffn_dw_b, ffn_w_down, norm_ffn, norm_final):
    depth = w_in.shape[0]
    row = lambda v: v.reshape(1, -1)
    tm = TIME_TILE
    win, cwo, mwo = _pack_bf16(w_in), _pack_bf16(conv_w_out), _pack_bf16(mix_w_out)
    pw = _pack_bf16(pool_w.reshape(depth, -1, pool_w.shape[-1]))
    wup, wdn = _pack_bf16(ffn_w_up), _pack_bf16(ffn_w_down)
    for l in range(depth):
        mixer_args = (row(norm_mix[l]), _Layer(win, l), row(b_in[l]), conv_dw_w[l],
                      row(conv_dw_b[l]), row(conv_ln_g[l]), row(conv_ln_b[l]), _Layer(cwo, l),
                      _Layer(pw, l), row(pool_scale[l]), _Layer(mwo, l))
        ffn_args = (row(norm_ffn[l]), _Layer(wup, l), ffn_dw_w[l], row(ffn_dw_b[l]),
                    _Layer(wdn, l), row(norm_final))
        x = _layer(x, mixer_args, ffn_args, tm=tm, final_norm=(l == depth - 1))
    return x
```

```python
import functools
import math

import jax
import jax.numpy as jnp
from jax import lax
from jax.experimental import pallas as pl
from jax.experimental.pallas import tpu as pltpu

LANES = 128
SUBLANES = 8
MXU_COLS = 256
VMEM_LIMIT_BYTES = 60 * 1024 * 1024

EPS = 1e-6
CONV_KERNEL = 31
POOL_WINDOWS = (2, 4, 8, 16)
FFN_KERNEL = 3

CONV_HALO = 32
POOL_HALO = 16
FFN_HALO = 8
PACKED_ROWS = 16
CONV_GROUP_ROWS = 32
PACK_BLOCK_BYTES = 6 * 1024 * 1024
TIME_TILE = 256
STRIPS_PER_CHUNK = MXU_COLS // LANES

GELU_C = math.sqrt(2.0 / math.pi)


def _rmsnorm(x, g):
    ms = jnp.mean(x * x, axis=-1, keepdims=True)
    return x * lax.rsqrt(ms + EPS) * g


def _sigmoid(x):
    return 0.5 + 0.5 * jnp.tanh(0.5 * x)


def _gelu_tanh(x):
    hx = 0.5 * x
    return hx + hx * jnp.tanh(x * (GELU_C + (0.044715 * GELU_C) * (x * x)))


def _dot(a, b):
    return jnp.dot(a, b, preferred_element_type=jnp.float32)


def _pack_kernel(w_ref, o_ref):
    o_ref[...] = pltpu.bitcast(w_ref[...].astype(jnp.bfloat16), jnp.int32)


def _pack_bf16(w):
    n_layers, k, n = w.shape
    bk = max(b for b in range(PACKED_ROWS, k + 1, PACKED_ROWS)
             if k % b == 0 and b * n * 4 <= PACK_BLOCK_BYTES)
    return pl.pallas_call(
        _pack_kernel,
        grid=(n_layers, k // bk),
        in_specs=[pl.BlockSpec((None, bk, n), lambda l, i: (l, i, 0))],
        out_specs=pl.BlockSpec((None, bk // 2, n), lambda l, i: (l, i, 0)),
        out_shape=jax.ShapeDtypeStruct((n_layers, k // 2, n), jnp.int32),
        compiler_params=pltpu.CompilerParams(
            dimension_semantics=("arbitrary", "arbitrary"),
            vmem_limit_bytes=VMEM_LIMIT_BYTES),
        name="pack_bf16",
    )(w)


def _unpack(packed):
    return pltpu.bitcast(packed, jnp.bfloat16)


def _vreg_row(ref, k, cs):
    return jnp.broadcast_to(ref[k:k + 1, cs], (SUBLANES, LANES))


class _MixerStage:
    def __init__(self, x_tile, out_tile, j, weights, scratch, *, tm, d):
        (self.nrm_ref, self.win_ref, self.bin_ref, self.dww_ref, self.dwb_ref, self.lng_ref,
         self.lnb_ref, self.cwo_ref, self.pw_ref, self.ps_ref, self.mwo_ref) = weights
        self.n_chunks = d // MXU_COLS
        assert self.n_chunks == len(POOL_WINDOWS)
        self.u_ext, scratch = scratch[:self.n_chunks], scratch[self.n_chunks:]
        self.p_ext, scratch = scratch[:self.n_chunks], scratch[self.n_chunks:]
        (self.h_scr, self.conv_scr, self.pooled_scr, self.ga_scr,
         self.gb_scr, self.act_scr, self.pooled_bf, self.merged_scr) = scratch
        self.x_tile, self.out_tile, self.j, self.tm, self.d = x_tile, out_tile, j, tm, d

    def zero_halos(self):
        for q in range(self.n_chunks):
            self.u_ext[q][:, 0:CONV_HALO, :] = jnp.zeros((STRIPS_PER_CHUNK, CONV_HALO, LANES), jnp.float32)
            self.p_ext[q][:, 0:POOL_HALO, :] = jnp.zeros((STRIPS_PER_CHUNK, POOL_HALO, LANES), jnp.float32)

    def normalize_input(self, x_tile):
        self.h_scr[...] = _rmsnorm(x_tile[...], self.nrm_ref[...]).astype(jnp.bfloat16)

    def _proj(self, col0, cols):
        sl = slice(col0 + cols.start, col0 + cols.stop)
        return _dot(self.h_scr[...], _unpack(self.win_ref[:, sl])) + self.bin_ref[:, sl]

    def project_conv_in(self, q):
        tm, d = self.tm, self.d
        cols = slice(q * MXU_COLS, (q + 1) * MXU_COLS)
        u = self._proj(0, cols) * _sigmoid(self._proj(d, cols))
        for s in range(STRIPS_PER_CHUNK):
            self.u_ext[q][s, CONV_HALO:CONV_HALO + tm, :] = u[:, s * LANES:(s + 1) * LANES]

    def project_pool_in(self, q):
        tm, d = self.tm, self.d
        pin = self._proj(2 * d, slice(q * MXU_COLS, (q + 1) * MXU_COLS))
        for s in range(STRIPS_PER_CHUNK):
            self.p_ext[q][s, POOL_HALO:POOL_HALO + tm, :] = pin[:, s * LANES:(s + 1) * LANES]

    def conv_piece(self, q, s, r0):
        conv_base = CONV_HALO - (CONV_KERNEL - 1)
        c = q * STRIPS_PER_CHUNK + s
        cs = slice(c * LANES, (c + 1) * LANES)
        ws = [_vreg_row(self.dww_ref, k, cs) for k in range(CONV_KERNEL)]
        blocks = range(r0, r0 + CONV_GROUP_ROWS, SUBLANES)
        accs = {r: _vreg_row(self.dwb_ref, 0, cs) for r in blocks}
        for lo in range(r0 + conv_base, r0 + conv_base + CONV_GROUP_ROWS + CONV_KERNEL - 1):
            taps = [(r, lo - conv_base - r) for r in blocks
                    if 0 <= lo - conv_base - r < CONV_KERNEL]
            win = self.u_ext[q][s, lo:lo + SUBLANES, :]
            for r, k in taps:
                accs[r] = accs[r] + ws[k] * win
        for r in blocks:
            self.conv_scr[r:r + SUBLANES, cs] = accs[r]

    def pool_piece(self, q, s):
        tm = self.tm
        row_iota = lax.broadcasted_iota(jnp.int32, (SUBLANES, LANES), 0)
        window = POOL_WINDOWS[q]
        c = q * STRIPS_PER_CHUNK + s
        cs = slice(c * LANES, (c + 1) * LANES)
        for r in range(0, tm, SUBLANES):
            tok = self.p_ext[q][s, r + POOL_HALO:r + POOL_HALO + SUBLANES, :]
            acc = tok
            for back in range(1, window):
                lo = r + POOL_HALO - back
                acc = acc + self.p_ext[q][s, lo:lo + SUBLANES, :]
            pos1 = (self.j * tm + (r + 1) + row_iota).astype(jnp.float32)
            cnt = jnp.minimum(pos1, float(window))
            self.pooled_scr[r:r + SUBLANES, cs] = acc / cnt - tok

    def gate_logits(self, which, n):
        col0, dst = ((3 * self.d, self.ga_scr), (4 * self.d, self.gb_scr))[which]
        cols = slice(n * MXU_COLS, (n + 1) * MXU_COLS)
        dst[:, cols] = _dot(self.h_scr[...],
                            _unpack(self.win_ref[:, col0 + cols.start:col0 + cols.stop]))

    def _gate(self, which, cols):
        col0, src = ((3 * self.d, self.ga_scr), (4 * self.d, self.gb_scr))[which]
        return _sigmoid(src[:, cols] + self.bin_ref[:, col0 + cols.start:col0 + cols.stop])

    def carry_halos(self):
        tm = self.tm
        for q in range(self.n_chunks):
            self.u_ext[q][:, 0:CONV_HALO, :] = self.u_ext[q][:, tm:tm + CONV_HALO, :]
            self.p_ext[q][:, 0:POOL_HALO, :] = self.p_ext[q][:, tm:tm + POOL_HALO, :]

    def normalize_piece(self, r0):
        rows = slice(r0, r0 + CONV_GROUP_ROWS)
        y = self.conv_scr[rows, :]
        mu = jnp.mean(y, axis=-1, keepdims=True)
        yc = y - mu
        var = jnp.mean(yc * yc, axis=-1, keepdims=True)
        yn = yc * lax.rsqrt(var + EPS) * self.lng_ref[...] + self.lnb_ref[...]
        self.act_scr[rows, :] = (yn * _sigmoid(yn)).astype(jnp.bfloat16)
        self.pooled_bf[rows, :] = self.pooled_scr[rows, :].astype(jnp.bfloat16)

    def merge_chunk(self, n):
        cols = slice(n * MXU_COLS, (n + 1) * MXU_COLS)
        ya = _dot(self.act_scr[...], _unpack(self.cwo_ref[:, cols]))
        group_rows = slice(cols.start // 2, cols.stop // 2)
        yb = _dot(self.pooled_bf[:, cols], _unpack(self.pw_ref[group_rows, :])) * self.ps_ref[:, cols]
        merged = self._gate(0, cols) * ya + self._gate(1, cols) * yb
        self.merged_scr[:, cols] = merged.astype(jnp.bfloat16)

    def out_chunk(self, n):
        cols = slice(n * MXU_COLS, (n + 1) * MXU_COLS)
        self.out_tile[:, cols] = self.x_tile[:, cols] + _dot(
            self.merged_scr[...], _unpack(self.mwo_ref[:, cols]))


class _FfnStage:
    def __init__(self, x_tile, out_tile, weights, scratch, *, tm, d, dff, final_norm):
        self.nrm_ref, self.wup_ref, self.dww_ref, self.dwb_ref, self.wdn_ref, self.nf_ref = weights
        self.n_chunks = dff // MXU_COLS
        self.u_ext, (self.h_scr, self.act_scr) = scratch[:self.n_chunks], scratch[self.n_chunks:]
        self.x_tile, self.out_tile = x_tile, out_tile
        self.tm, self.d, self.dff, self.final_norm = tm, d, dff, final_norm
        self.y = None

    def zero_halos(self):
        for q in range(self.n_chunks):
            self.u_ext[q][:, 0:FFN_HALO, :] = jnp.zeros((2 * STRIPS_PER_CHUNK, FFN_HALO, LANES), jnp.float32)

    def normalize_input(self):
        self.h_scr[...] = _rmsnorm(self.x_tile[...], self.nrm_ref[...]).astype(jnp.bfloat16)

    def begin(self):
        self.y = self.x_tile[...]

    def _conv3(self, q, local, r):
        base = FFN_HALO - (FFN_KERNEL - 1)
        is_gate, s = divmod(local, STRIPS_PER_CHUNK)
        col0 = is_gate * self.dff + (q * STRIPS_PER_CHUNK + s) * LANES
        cs = slice(col0, col0 + LANES)
        strip = self.u_ext[q]
        acc = (_vreg_row(self.dww_ref, 0, cs) * strip[local, r + base:r + base + SUBLANES, :]
               + _vreg_row(self.dwb_ref, 0, cs))
        for k in range(1, FFN_KERNEL):
            lo = r + base + k
            acc = acc + _vreg_row(self.dww_ref, k, cs) * strip[local, lo:lo + SUBLANES, :]
        return acc

    def up_project(self, q):
        tm, dff = self.tm, self.dff
        cols = slice(q * MXU_COLS, (q + 1) * MXU_COLS)
        a = _dot(self.h_scr[...], _unpack(self.wup_ref[:, cols]))
        g = _dot(self.h_scr[...], _unpack(self.wup_ref[:, dff + cols.start:dff + cols.stop]))
        for s in range(STRIPS_PER_CHUNK):
            ls = slice(s * LANES, (s + 1) * LANES)
            self.u_ext[q][s, FFN_HALO:FFN_HALO + tm, :] = a[:, ls]
            self.u_ext[q][STRIPS_PER_CHUNK + s, FFN_HALO:FFN_HALO + tm, :] = g[:, ls]

    def activate_piece(self, q, s):
        tm = self.tm
        c = q * STRIPS_PER_CHUNK + s
        for r in range(0, tm, PACKED_ROWS):
            rows = [_gelu_tanh(self._conv3(q, s, r + rr))
                    * self._conv3(q, STRIPS_PER_CHUNK + s, r + rr)
                    for rr in range(0, PACKED_ROWS, SUBLANES)]
            self.act_scr[r:r + PACKED_ROWS, c * LANES:(c + 1) * LANES] = (
                jnp.concatenate(rows, axis=0).astype(jnp.bfloat16))
        for local in (s, STRIPS_PER_CHUNK + s):
            self.u_ext[q][local, 0:FFN_HALO, :] = self.u_ext[q][local, tm:tm + FFN_HALO, :]

    def down_project(self, q):
        cols = slice(q * MXU_COLS, (q + 1) * MXU_COLS)
        packed_rows = slice(cols.start // 2, cols.stop // 2)
        self.y = self.y + _dot(self.act_scr[:, cols], _unpack(self.wdn_ref[packed_rows, :]))

    def epilogue(self):
        y = self.y
        if self.final_norm:
            y = _rmsnorm(y, self.nf_ref[...])
        self.out_tile[...] = y


class _Item:
    def __init__(self, name, mxu, emit, needs=(), valu=0, urgent=False):
        self.name, self.mxu, self.emit, self.needs, self.valu = name, mxu, emit, tuple(needs), valu
        self.urgent = urgent


def _emit_interleaved(plan, pieces):
    total_mxu = sum(it.mxu for it in plan)
    total_valu = sum(it.valu for it in plan) + sum(it.valu for it in pieces)
    done, pending = set(), list(pieces)
    spent = {"mxu": 0, "valu": 0}

    def emit(item):
        assert all(n in done for n in item.needs), (item.name, item.needs)
        item.emit()
        done.add(item.name)
        spent["mxu"] += item.mxu
        spent["valu"] += item.valu

    def emit_piece(name):
        piece = next(p for p in pending if p.name == name)
        pending.remove(piece)
        for need in piece.needs:
            if need not in done:
                emit_piece(need)
        emit(piece)

    for item in plan:
        for name in item.needs:
            if name not in done:
                emit_piece(name)
        emit(item)
        while True:
            ready = [p for p in pending if all(n in done for n in p.needs)]
            urgent = [p for p in ready if p.urgent]
            behind = spent["valu"] * total_mxu < spent["mxu"] * total_valu
            if urgent:
                emit_piece(urgent[0].name)
            elif ready and behind:
                emit_piece(ready[0].name)
            else:
                break
    for piece in list(pending):
        emit_piece(piece.name)


def _layer_kernel(*refs, tm, d, dff, tiles_per_seq, n_tiles, final_norm):
    x_ref, mixer_w, ffn_w, o_ref = refs[0], refs[1:12], refs[12:18], refs[18]
    n_mixer_scr = 2 * (d // MXU_COLS) + 8
    mixer_scr, x1_scr, ffn_scr = refs[19:19 + n_mixer_scr], refs[19 + n_mixer_scr], refs[20 + n_mixer_scr:]
    i = pl.program_id(0)
    jm = lax.rem(jnp.minimum(i, n_tiles - 1), tiles_per_seq)
    mixer = _MixerStage(x_ref.at[0], x1_scr, jm, mixer_w, mixer_scr, tm=tm, d=d)
    ffn = _FfnStage(x1_scr, o_ref.at[0], ffn_w, ffn_scr, tm=tm, d=d, dff=dff, final_norm=final_norm)

    @pl.when(i == 0)
    def _():
        x1_scr[...] = jnp.zeros((tm, d), jnp.float32)
        ffn.zero_halos()

    @pl.when(jm == 0)
    def _():
        mixer.zero_halos()

    @pl.when(lax.rem(i + tiles_per_seq - 1, tiles_per_seq) == 0)
    def _():
        ffn.zero_halos()

    ffn.begin()
    ffn.normalize_input()
    mixer.normalize_input(x_ref.at[0])

    P = functools.partial
    nq, nk = mixer.n_chunks, ffn.n_chunks
    row_groups = range(0, tm, CONV_GROUP_ROWS)
    dot_cycles = tm
    conv_names = [f"conv{q}.{s}.{r0}" for q in range(nq) for s in range(STRIPS_PER_CHUNK)
                  for r0 in row_groups]
    pool_names = [f"pool{q}.{s}" for q in range(nq) for s in range(STRIPS_PER_CHUNK)]
    norm_names = [f"norm{r0}" for r0 in row_groups]

    def conv_in(q):
        return _Item(f"PV{q}", 2 * dot_cycles, P(mixer.project_conv_in, q), valu=tm // 2)

    def pool_in(q):
        return _Item(f"PP{q}", dot_cycles, P(mixer.project_pool_in, q))

    def gates(which, ns):
        return [_Item(f"G{which}.{n}", dot_cycles, P(mixer.gate_logits, which, n)) for n in ns]

    def up(k):
        return _Item(f"UP{k}", 2 * dot_cycles, P(ffn.up_project, k))

    def down(k):
        needs = [f"act{k}.{s}" for s in range(STRIPS_PER_CHUNK)]
        return _Item(f"DN{k}", dot_cycles, P(ffn.down_project, k), needs=needs, valu=tm // 4)

    plan = [conv_in(0), pool_in(0), up(0), conv_in(1), pool_in(1), up(1)]
    mixer_fill = [gates(0, (0, 1)), [conv_in(2), pool_in(2)], gates(0, (2, 3)),
                  [conv_in(3), pool_in(3)], gates(1, (0, 1)), gates(1, (2, 3))]
    for k in range(2, nk + 2):
        if mixer_fill:
            plan += mixer_fill.pop(0)
        if k < nk:
            plan.append(up(k))
        plan.append(down(k - 2))
    assert not mixer_fill
    plan += [_Item(f"M{n}", dot_cycles * 5 // 4, P(mixer.merge_chunk, n), needs=norm_names,
                   valu=3 * tm // 4) for n in range(nq)]
    plan += [_Item(f"O{n}", dot_cycles, P(mixer.out_chunk, n), valu=tm // 8) for n in range(nq)]

    pieces = []
    for k in range(nk):
        pieces += [_Item(f"act{k}.{s}", 0, P(ffn.activate_piece, k, s), needs=[f"UP{k}"], valu=3 * tm // 4,
                         urgent=True) for s in range(STRIPS_PER_CHUNK)]
    pieces.append(_Item("ffn_out", 0, ffn.epilogue, needs=[f"DN{nk - 1}"], valu=tm))
    for q in range(nq):
        for s in range(STRIPS_PER_CHUNK):
            pieces += [_Item(f"conv{q}.{s}.{r0}", 0, P(mixer.conv_piece, q, s, r0), needs=[f"PV{q}"],
                             valu=5 * CONV_GROUP_ROWS // 2) for r0 in row_groups]
            pieces.append(_Item(f"pool{q}.{s}", 0, P(mixer.pool_piece, q, s), needs=[f"PP{q}"], valu=tm // 4))
    pieces.append(_Item("carry", 0, mixer.carry_halos, needs=conv_names + pool_names, valu=tm // 8))
    pieces += [_Item(f"norm{r0}", 0, P(mixer.normalize_piece, r0), needs=conv_names + pool_names,
                     valu=7 * CONV_GROUP_ROWS // 2) for r0 in row_groups]
    _emit_interleaved(plan, pieces)


class _Layer:
    def __init__(self, stacked, layer):
        self.stacked, self.layer = stacked, layer


def _resident(arg):
    if isinstance(arg, _Layer):
        tail = arg.stacked.shape[1:]
        index = (arg.layer,) + (0,) * len(tail)
        return pl.BlockSpec((None,) + tail, lambda i: index, pipeline_mode=pl.Buffered(1))
    zeros = (0,) * arg.ndim
    return pl.BlockSpec(arg.shape, lambda i: zeros, pipeline_mode=pl.Buffered(1))


def _operand(arg):
    return arg.stacked if isinstance(arg, _Layer) else arg


def _layer(x, mixer_args, ffn_args, *, tm, final_norm):
    bsz, seq, d = x.shape
    dff = 2 * ffn_args[4].stacked.shape[1]
    tiles_per_seq = seq // tm
    n_tiles = bsz * tiles_per_seq

    def tile_index(t):
        return (t // tiles_per_seq, t % tiles_per_seq, 0)

    x_spec = pl.BlockSpec((1, tm, d), lambda i: tile_index(jnp.minimum(i, n_tiles - 1)))
    o_spec = pl.BlockSpec((1, tm, d), lambda i: tile_index(jnp.maximum(i - 1, 0)))
    args = tuple(mixer_args) + tuple(ffn_args)
    f32_tile = pltpu.VMEM((tm, d), jnp.float32)
    bf16_tile = pltpu.VMEM((tm, d), jnp.bfloat16)
    return pl.pallas_call(
        functools.partial(_layer_kernel, tm=tm, d=d, dff=dff, tiles_per_seq=tiles_per_seq,
                          n_tiles=n_tiles, final_norm=final_norm),
        grid=(n_tiles + 1,),
        in_specs=[x_spec] + [_resident(a) for a in args],
        out_specs=o_spec,
        out_shape=jax.ShapeDtypeStruct(x.shape, x.dtype),
        scratch_shapes=(
            [pltpu.VMEM((STRIPS_PER_CHUNK, CONV_HALO + tm, LANES), jnp.float32)] * (d // MXU_COLS)
            + [pltpu.VMEM((STRIPS_PER_CHUNK, POOL_HALO + tm, LANES), jnp.float32)] * (d // MXU_COLS)
            + [bf16_tile,
               f32_tile, f32_tile, f32_tile, f32_tile,
               bf16_tile, bf16_tile, bf16_tile,
               f32_tile]
            + [pltpu.VMEM((2 * STRIPS_PER_CHUNK, FFN_HALO + tm, LANES), jnp.float32)] * (dff // MXU_COLS)
            + [bf16_tile,
               pltpu.VMEM((tm, dff), jnp.bfloat16)]),
        compiler_params=pltpu.CompilerParams(
            dimension_semantics=("arbitrary",),
            vmem_limit_bytes=VMEM_LIMIT_BYTES),
        name="layer",
    )(x, *[_operand(a) for a in args])


def kernel(x, w_in, b_in, conv_dw_w, conv_dw_b, conv_ln_g, conv_ln_b, conv_w_out, pool_w, pool_scale, mix_w_out, norm_mix, ffn_w_up, ffn_dw_w, ffn_dw_b, ffn_w_down, norm_ffn, norm_final):
    depth = w_in.shape[0]
    row = lambda v: v.reshape(1, -1)
    tm = TIME_TILE
    win, cwo, mwo = _pack_bf16(w_in), _pack_bf16(conv_w_out), _pack_bf16(mix_w_out)
    pw = _pack_bf16(pool_w.reshape(depth, -1, pool_w.shape[-1]))
    wup, wdn = _pack_bf16(ffn_w_up), _pack_bf16(ffn_w_down)
    for l in range(depth):
        mixer_args = (row(norm_mix[l]), _Layer(win, l), row(b_in[l]), conv_dw_w[l],
                      row(conv_dw_b[l]), row(conv_ln_g[l]), row(conv_ln_b[l]), _Layer(cwo, l),
                      _Layer(pw, l), row(pool_scale[l]), _Layer(mwo, l))
        ffn_args = (row(norm_ffn[l]), _Layer(wup, l), ffn_dw_w[l], row(ffn_dw_b[l]),
                    _Layer(wdn, l), row(norm_final))
        x = _layer(x, mixer_args, ffn_args, tm=tm, final_norm=(l == depth - 1))
    return x
```

```python
import functools
import math

import jax
import jax.numpy as jnp
from jax import lax
from jax.experimental import pallas as pl
from jax.experimental.pallas import tpu as pltpu

LANES = 128
SUBLANES = 8
MXU_COLS = 256
VMEM_LIMIT_BYTES = 60 * 1024 * 1024

EPS = 1e-6
CONV_KERNEL = 31
POOL_WINDOWS = (2, 4, 8, 16)
FFN_KERNEL = 3

CONV_HALO = 32
POOL_HALO = 16
FFN_HALO = 8
PACKED_ROWS = 16
CONV_GROUP_ROWS = 32
PACK_BLOCK_BYTES = 6 * 1024 * 1024
TIME_TILE = 256
STRIPS_PER_CHUNK = MXU_COLS // LANES

GELU_C = math.sqrt(2.0 / math.pi)


def _rmsnorm(x, g):
    ms = jnp.mean(x * x, axis=-1, keepdims=True)
    return x * lax.rsqrt(ms + EPS) * g


def _sigmoid(x):
    return 0.5 + 0.5 * jnp.tanh(0.5 * x)


def _gelu_tanh(x):
    hx = 0.5 * x
    return hx + hx * jnp.tanh(x * (GELU_C + (0.044715 * GELU_C) * (x * x)))


def _dot(a, b):
    return jnp.dot(a, b, preferred_element_type=jnp.float32)


def _pack_kernel(w_ref, o_ref):
    o_ref[...] = pltpu.bitcast(w_ref[...].astype(jnp.bfloat16), jnp.int32)


def _pack_bf16(w):
    n_layers, k, n = w.shape
    bk = max(b for b in range(PACKED_ROWS, k + 1, PACKED_ROWS)
             if k % b == 0 and b * n * 4 <= PACK_BLOCK_BYTES)
    return pl.pallas_call(
        _pack_kernel,
        grid=(n_layers, k // bk),
        in_specs=[pl.BlockSpec((None, bk, n), lambda l, i: (l, i, 0))],
        out_specs=pl.BlockSpec((None, bk // 2, n), lambda l, i: (l, i, 0)),
        out_shape=jax.ShapeDtypeStruct((n_layers, k // 2, n), jnp.int32),
        compiler_params=pltpu.CompilerParams(
            dimension_semantics=("arbitrary", "arbitrary"),
            vmem_limit_bytes=VMEM_LIMIT_BYTES),
        name="pack_bf16",
    )(w)


def _unpack(packed):
    return pltpu.bitcast(packed, jnp.bfloat16)


def _vreg_row(ref, k, cs):
    return jnp.broadcast_to(ref[k:k + 1, cs], (SUBLANES, LANES))


class _MixerStage:
    def __init__(self, x_tile, out_tile, j, weights, scratch, *, tm, d):
        (self.nrm_ref, self.win_ref, self.bin_ref, self.dww_ref, self.dwb_ref, self.lng_ref,
         self.lnb_ref, self.cwo_ref, self.pw_ref, self.ps_ref, self.mwo_ref) = weights
        self.n_chunks = d // MXU_COLS
        assert self.n_chunks == len(POOL_WINDOWS)
        self.u_ext, scratch = scratch[:self.n_chunks], scratch[self.n_chunks:]
        self.p_ext, scratch = scratch[:self.n_chunks], scratch[self.n_chunks:]
        (self.h_scr, self.conv_scr, self.pooled_scr, self.ga_scr,
         self.gb_scr, self.act_scr, self.pooled_bf, self.merged_scr) = scratch
        self.x_tile, self.out_tile, self.j, self.tm, self.d = x_tile, out_tile, j, tm, d

    def zero_halos(self):
        for q in range(self.n_chunks):
            self.u_ext[q][:, 0:CONV_HALO, :] = jnp.zeros((STRIPS_PER_CHUNK, CONV_HALO, LANES), jnp.float32)
            self.p_ext[q][:, 0:POOL_HALO, :] = jnp.zeros((STRIPS_PER_CHUNK, POOL_HALO, LANES), jnp.float32)

    def normalize_input(self, x_tile):
        self.h_scr[...] = _rmsnorm(x_tile[...], self.nrm_ref[...]).astype(jnp.bfloat16)

    def _proj(self, col0, cols):
        sl = slice(col0 + cols.start, col0 + cols.stop)
        return _dot(self.h_scr[...], _unpack(self.win_ref[:, sl])) + self.bin_ref[:, sl]

    def project_conv_in(self, q):
        tm, d = self.tm, self.d
        cols = slice(q * MXU_COLS, (q + 1) * MXU_COLS)
        u = self._proj(0, cols) * _sigmoid(self._proj(d, cols))
        for s in range(STRIPS_PER_CHUNK):
            self.u_ext[q][s, CONV_HALO:CONV_HALO + tm, :] = u[:, s * LANES:(s + 1) * LANES]

    def project_pool_in(self, q):
        tm, d = self.tm, self.d
        pin = self._proj(2 * d, slice(q * MXU_COLS, (q + 1) * MXU_COLS))
        for s in range(STRIPS_PER_CHUNK):
            self.p_ext[q][s, POOL_HALO:POOL_HALO + tm, :] = pin[:, s * LANES:(s + 1) * LANES]

    def conv_piece(self, q, s, r0):
        conv_base = CONV_HALO - (CONV_KERNEL - 1)
        c = q * STRIPS_PER_CHUNK + s
        cs = slice(c * LANES, (c + 1) * LANES)
        ws = [_vreg_row(self.dww_ref, k, cs) for k in range(CONV_KERNEL)]
        blocks = range(r0, r0 + CONV_GROUP_ROWS, SUBLANES)
        accs = {r: _vreg_row(self.dwb_ref, 0, cs) for r in blocks}
        for lo in range(r0 + conv_base, r0 + conv_base + CONV_GROUP_ROWS + CONV_KERNEL - 1):
            taps = [(r, lo - conv_base - r) for r in blocks
                    if 0 <= lo - conv_base - r < CONV_KERNEL]
            win = self.u_ext[q][s, lo:lo + SUBLANES, :]
            for r, k in taps:
                accs[r] = accs[r] + ws[k] * win
        for r in blocks:
            self.conv_scr[r:r + SUBLANES, cs] = accs[r]

    def pool_piece(self, q, s):
        tm = self.tm
        row_iota = lax.broadcasted_iota(jnp.int32, (SUBLANES, LANES), 0)
        window = POOL_WINDOWS[q]
        c = q * STRIPS_PER_CHUNK + s
        cs = slice(c * LANES, (c + 1) * LANES)
        for r in range(0, tm, SUBLANES):
            tok = self.p_ext[q][s, r + POOL_HALO:r + POOL_HALO + SUBLANES, :]
            acc = tok
            for back in range(1, window):
                lo = r + POOL_HALO - back
                acc = acc + self.p_ext[q][s, lo:lo + SUBLANES, :]
            pos1 = (self.j * tm + (r + 1) + row_iota).astype(jnp.float32)
            cnt = jnp.minimum(pos1, float(window))
            self.pooled_scr[r:r + SUBLANES, cs] = acc / cnt - tok

    def gate_logits(self, which, n):
        col0, dst = ((3 * self.d, self.ga_scr), (4 * self.d, self.gb_scr))[which]
        cols = slice(n * MXU_COLS, (n + 1) * MXU_COLS)
        dst[:, cols] = _dot(self.h_scr[...],
                            _unpack(self.win_ref[:, col0 + cols.start:col0 + cols.stop]))

    def _gate(self, which, cols):
        col0, src = ((3 * self.d, self.ga_scr), (4 * self.d, self.gb_scr))[which]
        return _sigmoid(src[:, cols] + self.bin_ref[:, col0 + cols.start:col0 + cols.stop])

    def carry_halos(self):
        tm = self.tm
        for q in range(self.n_chunks):
            self.u_ext[q][:, 0:CONV_HALO, :] = self.u_ext[q][:, tm:tm + CONV_HALO, :]
            self.p_ext[q][:, 0:POOL_HALO, :] = self.p_ext[q][:, tm:tm + POOL_HALO, :]

    def normalize_piece(self, r0):
        rows = slice(r0, r0 + CONV_GROUP_ROWS)
        y = self.conv_scr[rows, :]
        mu = jnp.mean(y, axis=-1, keepdims=True)
        yc = y - mu
        var = jnp.mean(yc * yc, axis=-1, keepdims=True)
        yn = yc * lax.rsqrt(var + EPS) * self.lng_ref[...] + self.lnb_ref[...]
        self.act_scr[rows, :] = (yn * _sigmoid(yn)).astype(jnp.bfloat16)
        self.pooled_bf[rows, :] = self.pooled_scr[rows, :].astype(jnp.bfloat16)

    def merge_chunk(self, n):
        cols = slice(n * MXU_COLS, (n + 1) * MXU_COLS)
        ya = _dot(self.act_scr[...], _unpack(self.cwo_ref[:, cols]))
        group_rows = slice(cols.start // 2, cols.stop // 2)
        yb = _dot(self.pooled_bf[:, cols], _unpack(self.pw_ref[group_rows, :])) * self.ps_ref[:, cols]
        merged = self._gate(0, cols) * ya + self._gate(1, cols) * yb
        self.merged_scr[:, cols] = merged.astype(jnp.bfloat16)

    def out_chunk(self, n):
        cols = slice(n * MXU_COLS, (n + 1) * MXU_COLS)
        self.out_tile[:, cols] = self.x_tile[:, cols] + _dot(
            self.merged_scr[...], _unpack(self.mwo_ref[:, cols]))


class _FfnStage:
    def __init__(self, x_tile, out_tile, weights, scratch, *, tm, d, dff, final_norm):
        self.nrm_ref, self.wup_ref, self.dww_ref, self.dwb_ref, self.wdn_ref, self.nf_ref = weights
        self.n_chunks = dff // MXU_COLS
        self.u_ext, (self.h_scr, self.act_scr) = scratch[:self.n_chunks], scratch[self.n_chunks:]
        self.x_tile, self.out_tile = x_tile, out_tile
        self.tm, self.d, self.dff, self.final_norm = tm, d, dff, final_norm
        self.y = None

    def zero_halos(self):
        for q in range(self.n_chunks):
            self.u_ext[q][:, 0:FFN_HALO, :] = jnp.zeros((2 * STRIPS_PER_CHUNK, FFN_HALO, LANES), jnp.float32)

    def normalize_input(self):
        self.h_scr[...] = _rmsnorm(self.x_tile[...], self.nrm_ref[...]).astype(jnp.bfloat16)

    def begin(self):
        self.y = self.x_tile[...]

    def _conv3(self, q, local, r):
        base = FFN_HALO - (FFN_KERNEL - 1)
        is_gate, s = divmod(local, STRIPS_PER_CHUNK)
        col0 = is_gate * self.dff + (q * STRIPS_PER_CHUNK + s) * LANES
        cs = slice(col0, col0 + LANES)
        strip = self.u_ext[q]
        acc = (_vreg_row(self.dww_ref, 0, cs) * strip[local, r + base:r + base + SUBLANES, :]
               + _vreg_row(self.dwb_ref, 0, cs))
        for k in range(1, FFN_KERNEL):
            lo = r + base + k
            acc = acc + _vreg_row(self.dww_ref, k, cs) * strip[local, lo:lo + SUBLANES, :]
        return acc

    def up_project(self, q):
        tm, dff = self.tm, self.dff
        cols = slice(q * MXU_COLS, (q + 1) * MXU_COLS)
        a = _dot(self.h_scr[...], _unpack(self.wup_ref[:, cols]))
        g = _dot(self.h_scr[...], _unpack(self.wup_ref[:, dff + cols.start:dff + cols.stop]))
        for s in range(STRIPS_PER_CHUNK):
            ls = slice(s * LANES, (s + 1) * LANES)
            self.u_ext[q][s, FFN_HALO:FFN_HALO + tm, :] = a[:, ls]
            self.u_ext[q][STRIPS_PER_CHUNK + s, FFN_HALO:FFN_HALO + tm, :] = g[:, ls]

    def activate_piece(self, q, s):
        tm = self.tm
        c = q * STRIPS_PER_CHUNK + s
        for r in range(0, tm, PACKED_ROWS):
            rows = [_gelu_tanh(self._conv3(q, s, r + rr))
                    * self._conv3(q, STRIPS_PER_CHUNK + s, r + rr)
                    for rr in range(0, PACKED_ROWS, SUBLANES)]
            self.act_scr[r:r + PACKED_ROWS, c * LANES:(c + 1) * LANES] = (
                jnp.concatenate(rows, axis=0).astype(jnp.bfloat16))
        for local in (s, STRIPS_PER_CHUNK + s):
            self.u_ext[q][local, 0:FFN_HALO, :] = self.u_ext[q][local, tm:tm + FFN_HALO, :]

    def down_project(self, q):
        cols = slice(q * MXU_COLS, (q + 1) * MXU_COLS)
        packed_rows = slice(cols.start // 2, cols.stop // 2)
        self.y = self.y + _dot(self.act_scr[:, cols], _unpack(self.wdn_ref[packed_rows, :]))

    def epilogue(self):
        y = self.y
        if self.final_norm:
            y = _rmsnorm(y, self.nf_ref[...])
        self.out_tile[...] = y


class _Item:
    def __init__(self, name, mxu, emit, needs=(), valu=0, urgent=False):
        self.name, self.mxu, self.emit, self.needs, self.valu = name, mxu, emit, tuple(needs), valu
        self.urgent = urgent


def _emit_interleaved(plan, pieces):
    total_mxu = sum(it.mxu for it in plan)
    total_valu = sum(it.valu for it in plan) + sum(it.valu for it in pieces)
    done, pending = set(), list(pieces)
    spent = {"mxu": 0, "valu": 0}

    def emit(item):
        assert all(n in done for n in item.needs), (item.name, item.needs)
        item.emit()
        done.add(item.name)
        spent["mxu"] += item.mxu
        spent["valu"] += item.valu

    def emit_piece(name):
        piece = next(p for p in pending if p.name == name)
        pending.remove(piece)
        for need in piece.needs:
            if need not in done:
                emit_piece(need)
        emit(piece)

    for item in plan:
        for name in item.needs:
            if name not in done:
                emit_piece(name)
        emit(item)
        while True:
            ready = [p for p in pending if all(n in done for n in p.needs)]
            urgent = [p for p in ready if p.urgent]
            behind = spent["valu"] * total_mxu < spent["mxu"] * total_valu
            if urgent:
                emit_piece(urgent[0].name)
            elif ready and behind:
                emit_piece(ready[0].name)
            else:
                break
    for piece in list(pending):
        emit_piece(piece.name)


def _layer_kernel(*refs, tm, d, dff, tiles_per_seq, n_tiles, final_norm):
    x_ref, mixer_w, ffn_w, o_ref = refs[0], refs[1:12], refs[12:18], refs[18]
    n_mixer_scr = 2 * (d // MXU_COLS) + 8
    mixer_scr, x1_scr, ffn_scr = refs[19:19 + n_mixer_scr], refs[19 + n_mixer_scr], refs[20 + n_mixer_scr:]
    i = pl.program_id(0)
    jm = lax.rem(jnp.minimum(i, n_tiles - 1), tiles_per_seq)
    mixer = _MixerStage(x_ref.at[0], x1_scr, jm, mixer_w, mixer_scr, tm=tm, d=d)
    ffn = _FfnStage(x1_scr, o_ref.at[0], ffn_w, ffn_scr, tm=tm, d=d, dff=dff, final_norm=final_norm)

    @pl.when(i == 0)
    def _():
        x1_scr[...] = jnp.zeros((tm, d), jnp.float32)
        ffn.zero_halos()

    @pl.when(jm == 0)
    def _():
        mixer.zero_halos()

    @pl.when(lax.rem(i + tiles_per_seq - 1, tiles_per_seq) == 0)
    def _():
        ffn.zero_halos()

    ffn.begin()
    ffn.normalize_input()
    mixer.normalize_input(x_ref.at[0])

    P = functools.partial
    nq, nk = mixer.n_chunks, ffn.n_chunks
    row_groups = range(0, tm, CONV_GROUP_ROWS)
    dot_cycles = tm
    conv_names = [f"conv{q}.{s}.{r0}" for q in range(nq) for s in range(STRIPS_PER_CHUNK)
                  for r0 in row_groups]
    pool_names = [f"pool{q}.{s}" for q in range(nq) for s in range(STRIPS_PER_CHUNK)]
    norm_names = [f"norm{r0}" for r0 in row_groups]

    def conv_in(q):
        return _Item(f"PV{q}", 2 * dot_cycles, P(mixer.project_conv_in, q), valu=tm // 2)

    def pool_in(q):
        return _Item(f"PP{q}", dot_cycles, P(mixer.project_pool_in, q))

    def gates(which, ns):
        return [_Item(f"G{which}.{n}", dot_cycles, P(mixer.gate_logits, which, n)) for n in ns]

    def up(k):
        return _Item(f"UP{k}", 2 * dot_cycles, P(ffn.up_project, k))

    def down(k):
        needs = [f"act{k}.{s}" for s in range(STRIPS_PER_CHUNK)]
        return _Item(f"DN{k}", dot_cycles, P(ffn.down_project, k), needs=needs, valu=tm // 4)

    plan = [conv_in(0), pool_in(0), up(0), conv_in(1), pool_in(1), up(1)]
    plan += gates(0, (0, 1)) + [up(2), conv_in(2), pool_in(2), up(3)] + gates(0, (2, 3))
    plan += [up(4), conv_in(3), pool_in(3)] + gates(1, (0, 1, 2, 3))
    tail = ([_Item(f"M{n}", dot_cycles * 5 // 4, P(mixer.merge_chunk, n), needs=norm_names,
                   valu=3 * tm // 4) for n in range(nq)]
            + [_Item(f"O{n}", dot_cycles, P(mixer.out_chunk, n), valu=tm // 8) for n in range(nq)])
    first_late_up = 5
    for k in range(nk):
        if first_late_up + k < nk:
            plan.append(up(first_late_up + k))
        elif tail:
            plan.append(tail.pop(0))
        plan.append(down(k))
    plan += tail

    pieces = []
    for k in range(nk):
        pieces += [_Item(f"act{k}.{s}", 0, P(ffn.activate_piece, k, s), needs=[f"UP{k}"], valu=3 * tm // 4,
                         urgent=True) for s in range(STRIPS_PER_CHUNK)]
    pieces.append(_Item("ffn_out", 0, ffn.epilogue, needs=[f"DN{nk - 1}"], valu=tm))
    for q in range(nq):
        for s in range(STRIPS_PER_CHUNK):
            pieces += [_Item(f"conv{q}.{s}.{r0}", 0, P(mixer.conv_piece, q, s, r0), needs=[f"PV{q}"],
                             valu=5 * CONV_GROUP_ROWS // 2) for r0 in row_groups]
            pieces.append(_Item(f"pool{q}.{s}", 0, P(mixer.pool_piece, q, s), needs=[f"PP{q}"], valu=tm // 4))
    pieces.append(_Item("carry", 0, mixer.carry_halos, needs=conv_names + pool_names, valu=tm // 8))
    pieces += [_Item(f"norm{r0}", 0, P(mixer.normalize_piece, r0), needs=conv_names + pool_names,
                     valu=7 * CONV_GROUP_ROWS // 2) for r0 in row_groups]
    _emit_interleaved(plan, pieces)


class _Layer:
    def __init__(self, stacked, layer):
        self.stacked, self.layer = stacked, layer


def _resident(arg):
    if isinstance(arg, _Layer):
        tail = arg.stacked.shape[1:]
        index = (arg.layer,) + (0,) * len(tail)
        return pl.BlockSpec((None,) + tail, lambda i: index, pipeline_mode=pl.Buffered(1))
    zeros = (0,) * arg.ndim
    return pl.BlockSpec(arg.shape, lambda i: zeros, pipeline_mode=pl.Buffered(1))


def _operand(arg):
    return arg.stacked if isinstance(arg, _Layer) else arg


def _layer(x, mixer_args, ffn_args, *, tm, final_norm):
    bsz, seq, d = x.shape
    dff = 2 * ffn_args[4].stacked.shape[1]
    tiles_per_seq = seq // tm
    n_tiles = bsz * tiles_per_seq

    def tile_index(t):
        return (t // tiles_per_seq, t % tiles_per_seq, 0)

    x_spec = pl.BlockSpec((1, tm, d), lambda i: tile_index(jnp.minimum(i, n_tiles - 1)))
    o_spec = pl.BlockSpec((1, tm, d), lambda i: tile_index(jnp.maximum(i - 1, 0)))
    args = tuple(mixer_args) + tuple(ffn_args)
    f32_tile = pltpu.VMEM((tm, d), jnp.float32)
    bf16_tile = pltpu.VMEM((tm, d), jnp.bfloat16)
    return pl.pallas_call(
        functools.partial(_layer_kernel, tm=tm, d=d, dff=dff, tiles_per_seq=tiles_per_seq,
                          n_tiles=n_tiles, final_norm=final_norm),
        grid=(n_tiles + 1,),
        in_specs=[x_spec] + [_resident(a) for a in args],
        out_specs=o_spec,
        out_shape=jax.ShapeDtypeStruct(x.shape, x.dtype),
        scratch_shapes=(
            [pltpu.VMEM((STRIPS_PER_CHUNK, CONV_HALO + tm, LANES), jnp.float32)] * (d // MXU_COLS)
            + [pltpu.VMEM((STRIPS_PER_CHUNK, POOL_HALO + tm, LANES), jnp.float32)] * (d // MXU_COLS)
            + [bf16_tile,
               f32_tile, f32_tile, f32_tile, f32_tile,
               bf16_tile, bf16_tile, bf16_tile,
               f32_tile]
            + [pltpu.VMEM((2 * STRIPS_PER_CHUNK, FFN_HALO + tm, LANES), jnp.float32)] * (dff // MXU_COLS)
            + [bf16_tile,
               pltpu.VMEM((tm, dff), jnp.bfloat16)]),
        compiler_params=pltpu.CompilerParams(
            dimension_semantics=("arbitrary",),
            vmem_limit_bytes=VMEM_LIMIT_BYTES),
        name="layer",
    )(x, *[_operand(a) for a in args])


def kernel(x, w_in, b_in, conv_dw_w, conv_dw_b, conv_ln_g, conv_ln_b, conv_w_out, pool_w, pool_scale, mix_w_out, norm_mix, ffn_w_up, ffn_dw_w, ffn_dw_b, ffn_w_down, norm_ffn, norm_final):
    depth = w_in.shape[0]
    row = lambda v: v.reshape(1, -1)
    tm = TIME_TILE
    win, cwo, mwo = _pack_bf16(w_in), _pack_bf16(conv_w_out), _pack_bf16(mix_w_out)
    pw = _pack_bf16(pool_w.reshape(depth, -1, pool_w.shape[-1]))
    wup, wdn = _pack_bf16(ffn_w_up), _pack_bf16(ffn_w_down)
    for l in range(depth):
        mixer_args = (row(norm_mix[l]), _Layer(win, l), row(b_in[l]), conv_dw_w[l],
                      row(conv_dw_b[l]), row(conv_ln_g[l]), row(conv_ln_b[l]), _Layer(cwo, l),
                      _Layer(pw, l), row(pool_scale[l]), _Layer(mwo, l))
        ffn_args = (row(norm_ffn[l]), _Layer(wup, l), ffn_dw_w[l], row(ffn_dw_b[l]),
                    _Layer(wdn, l), row(norm_final))
        x = _layer(x, mixer_args, ffn_args, tm=tm, final_norm=(l == depth - 1))
    return x
```

```python
import functools
import math

import jax
import jax.numpy as jnp
from jax import lax
from jax.experimental import pallas as pl
from jax.experimental.pallas import tpu as pltpu

LANES = 128
SUBLANES = 8
MXU_COLS = 256
VMEM_LIMIT_BYTES = 60 * 1024 * 1024

EPS = 1e-6
CONV_KERNEL = 31
POOL_WINDOWS = (2, 4, 8, 16)
FFN_KERNEL = 3

CONV_HALO = 32
POOL_HALO = 16
FFN_HALO = 8
PACKED_ROWS = 16
CONV_GROUP_ROWS = 32
PACK_BLOCK_BYTES = 6 * 1024 * 1024
TIME_TILE = 256
STRIPS_PER_CHUNK = MXU_COLS // LANES

GELU_C = math.sqrt(2.0 / math.pi)


def _rmsnorm(x, g):
    ms = jnp.mean(x * x, axis=-1, keepdims=True)
    return x * lax.rsqrt(ms + EPS) * g


def _sigmoid(x):
    return 0.5 + 0.5 * jnp.tanh(0.5 * x)


def _gelu_tanh(x):
    hx = 0.5 * x
    return hx + hx * jnp.tanh(x * (GELU_C + (0.044715 * GELU_C) * (x * x)))


def _dot(a, b):
    return jnp.dot(a, b, preferred_element_type=jnp.float32)


def _pack_kernel(w_ref, o_ref):
    o_ref[...] = pltpu.bitcast(w_ref[...].astype(jnp.bfloat16), jnp.int32)


def _pack_bf16(w):
    n_layers, k, n = w.shape
    bk = max(b for b in range(PACKED_ROWS, k + 1, PACKED_ROWS)
             if k % b == 0 and b * n * 4 <= PACK_BLOCK_BYTES)
    return pl.pallas_call(
        _pack_kernel,
        grid=(n_layers, k // bk),
        in_specs=[pl.BlockSpec((None, bk, n), lambda l, i: (l, i, 0))],
        out_specs=pl.BlockSpec((None, bk // 2, n), lambda l, i: (l, i, 0)),
        out_shape=jax.ShapeDtypeStruct((n_layers, k // 2, n), jnp.int32),
        compiler_params=pltpu.CompilerParams(
            dimension_semantics=("arbitrary", "arbitrary"),
            vmem_limit_bytes=VMEM_LIMIT_BYTES),
        name="pack_bf16",
    )(w)


def _unpack(packed):
    return pltpu.bitcast(packed, jnp.bfloat16)


def _vreg_row(ref, k, cs):
    return jnp.broadcast_to(ref[k:k + 1, cs], (SUBLANES, LANES))


class _MixerStage:
    def __init__(self, x_tile, out_tile, j, weights, scratch, *, tm, d):
        (self.nrm_ref, self.win_ref, self.bin_ref, self.dww_ref, self.dwb_ref, self.lng_ref,
         self.lnb_ref, self.cwo_ref, self.pw_ref, self.ps_ref, self.mwo_ref) = weights
        self.n_chunks = d // MXU_COLS
        assert self.n_chunks == len(POOL_WINDOWS)
        self.u_ext, scratch = scratch[:self.n_chunks], scratch[self.n_chunks:]
        self.p_ext, scratch = scratch[:self.n_chunks], scratch[self.n_chunks:]
        (self.h_scr, self.conv_scr, self.pooled_scr, self.ga_scr,
         self.gb_scr, self.act_scr, self.pooled_bf, self.merged_scr) = scratch
        self.x_tile, self.out_tile, self.j, self.tm, self.d = x_tile, out_tile, j, tm, d

    def zero_halos(self):
        for q in range(self.n_chunks):
            self.u_ext[q][:, 0:CONV_HALO, :] = jnp.zeros((STRIPS_PER_CHUNK, CONV_HALO, LANES), jnp.float32)
            self.p_ext[q][:, 0:POOL_HALO, :] = jnp.zeros((STRIPS_PER_CHUNK, POOL_HALO, LANES), jnp.float32)

    def normalize_input(self, x_tile):
        self.h_scr[...] = _rmsnorm(x_tile[...], self.nrm_ref[...]).astype(jnp.bfloat16)

    def _proj(self, col0, cols):
        sl = slice(col0 + cols.start, col0 + cols.stop)
        return _dot(self.h_scr[...], _unpack(self.win_ref[:, sl])) + self.bin_ref[:, sl]

    def project_conv_in(self, q):
        tm, d = self.tm, self.d
        cols = slice(q * MXU_COLS, (q + 1) * MXU_COLS)
        u = self._proj(0, cols) * _sigmoid(self._proj(d, cols))
        for s in range(STRIPS_PER_CHUNK):
            self.u_ext[q][s, CONV_HALO:CONV_HALO + tm, :] = u[:, s * LANES:(s + 1) * LANES]

    def project_pool_in(self, q):
        tm, d = self.tm, self.d
        pin = self._proj(2 * d, slice(q * MXU_COLS, (q + 1) * MXU_COLS))
        for s in range(STRIPS_PER_CHUNK):
            self.p_ext[q][s, POOL_HALO:POOL_HALO + tm, :] = pin[:, s * LANES:(s + 1) * LANES]

    def conv_piece(self, q, s, r0):
        conv_base = CONV_HALO - (CONV_KERNEL - 1)
        c = q * STRIPS_PER_CHUNK + s
        cs = slice(c * LANES, (c + 1) * LANES)
        ws = [_vreg_row(self.dww_ref, k, cs) for k in range(CONV_KERNEL)]
        blocks = range(r0, r0 + CONV_GROUP_ROWS, SUBLANES)
        accs = {r: _vreg_row(self.dwb_ref, 0, cs) for r in blocks}
        for lo in range(r0 + conv_base, r0 + conv_base + CONV_GROUP_ROWS + CONV_KERNEL - 1):
            taps = [(r, lo - conv_base - r) for r in blocks
                    if 0 <= lo - conv_base - r < CONV_KERNEL]
            win = self.u_ext[q][s, lo:lo + SUBLANES, :]
            for r, k in taps:
                accs[r] = accs[r] + ws[k] * win
        for r in blocks:
            self.conv_scr[r:r + SUBLANES, cs] = accs[r]

    def pool_piece(self, q, s):
        tm = self.tm
        row_iota = lax.broadcasted_iota(jnp.int32, (SUBLANES, LANES), 0)
        window = POOL_WINDOWS[q]
        c = q * STRIPS_PER_CHUNK + s
        cs = slice(c * LANES, (c + 1) * LANES)
        for r in range(0, tm, SUBLANES):
            tok = self.p_ext[q][s, r + POOL_HALO:r + POOL_HALO + SUBLANES, :]
            acc = tok
            for back in range(1, window):
                lo = r + POOL_HALO - back
                acc = acc + self.p_ext[q][s, lo:lo + SUBLANES, :]
            pos1 = (self.j * tm + (r + 1) + row_iota).astype(jnp.float32)
            cnt = jnp.minimum(pos1, float(window))
            self.pooled_scr[r:r + SUBLANES, cs] = acc / cnt - tok

    def gate_logits(self, which, n):
        col0, dst = ((3 * self.d, self.ga_scr), (4 * self.d, self.gb_scr))[which]
        cols = slice(n * MXU_COLS, (n + 1) * MXU_COLS)
        dst[:, cols] = _dot(self.h_scr[...],
                            _unpack(self.win_ref[:, col0 + cols.start:col0 + cols.stop]))

    def _gate(self, which, cols):
        col0, src = ((3 * self.d, self.ga_scr), (4 * self.d, self.gb_scr))[which]
        return _sigmoid(src[:, cols] + self.bin_ref[:, col0 + cols.start:col0 + cols.stop])

    def carry_halos(self):
        tm = self.tm
        for q in range(self.n_chunks):
            self.u_ext[q][:, 0:CONV_HALO, :] = self.u_ext[q][:, tm:tm + CONV_HALO, :]
            self.p_ext[q][:, 0:POOL_HALO, :] = self.p_ext[q][:, tm:tm + POOL_HALO, :]

    def normalize_piece(self, r0):
        rows = slice(r0, r0 + CONV_GROUP_ROWS)
        y = self.conv_scr[rows, :]
        mu = jnp.mean(y, axis=-1, keepdims=True)
        yc = y - mu
        var = jnp.mean(yc * yc, axis=-1, keepdims=True)
        yn = yc * lax.rsqrt(var + EPS) * self.lng_ref[...] + self.lnb_ref[...]
        self.act_scr[rows, :] = (yn * _sigmoid(yn)).astype(jnp.bfloat16)
        self.pooled_bf[rows, :] = self.pooled_scr[rows, :].astype(jnp.bfloat16)

    def merge_chunk(self, n):
        cols = slice(n * MXU_COLS, (n + 1) * MXU_COLS)
        ya = _dot(self.act_scr[...], _unpack(self.cwo_ref[:, cols]))
        group_rows = slice(cols.start // 2, cols.stop // 2)
        yb = _dot(self.pooled_bf[:, cols], _unpack(self.pw_ref[group_rows, :])) * self.ps_ref[:, cols]
        merged = self._gate(0, cols) * ya + self._gate(1, cols) * yb
        self.merged_scr[:, cols] = merged.astype(jnp.bfloat16)

    def out_chunk(self, n):
        cols = slice(n * MXU_COLS, (n + 1) * MXU_COLS)
        self.out_tile[:, cols] = self.x_tile[:, cols] + _dot(
            self.merged_scr[...], _unpack(self.mwo_ref[:, cols]))


class _FfnStage:
    def __init__(self, x_tile, out_tile, weights, scratch, *, tm, d, dff, final_norm):
        self.nrm_ref, self.wup_ref, self.dww_ref, self.dwb_ref, self.wdn_ref, self.nf_ref = weights
        self.n_chunks = dff // MXU_COLS
        self.u_ext, (self.h_scr, self.act_scr) = scratch[:self.n_chunks], scratch[self.n_chunks:]
        self.x_tile, self.out_tile = x_tile, out_tile
        self.tm, self.d, self.dff, self.final_norm = tm, d, dff, final_norm
        self.y = None

    def zero_halos(self):
        for q in range(self.n_chunks):
            self.u_ext[q][:, 0:FFN_HALO, :] = jnp.zeros((2 * STRIPS_PER_CHUNK, FFN_HALO, LANES), jnp.float32)

    def normalize_input(self):
        self.h_scr[...] = _rmsnorm(self.x_tile[...], self.nrm_ref[...]).astype(jnp.bfloat16)

    def begin(self):
        self.y = self.x_tile[...]

    def _conv3(self, q, local, r):
        base = FFN_HALO - (FFN_KERNEL - 1)
        is_gate, s = divmod(local, STRIPS_PER_CHUNK)
        col0 = is_gate * self.dff + (q * STRIPS_PER_CHUNK + s) * LANES
        cs = slice(col0, col0 + LANES)
        strip = self.u_ext[q]
        acc = (_vreg_row(self.dww_ref, 0, cs) * strip[local, r + base:r + base + SUBLANES, :]
               + _vreg_row(self.dwb_ref, 0, cs))
        for k in range(1, FFN_KERNEL):
            lo = r + base + k
            acc = acc + _vreg_row(self.dww_ref, k, cs) * strip[local, lo:lo + SUBLANES, :]
        return acc

    def up_project(self, q):
        tm, dff = self.tm, self.dff
        cols = slice(q * MXU_COLS, (q + 1) * MXU_COLS)
        a = _dot(self.h_scr[...], _unpack(self.wup_ref[:, cols]))
        g = _dot(self.h_scr[...], _unpack(self.wup_ref[:, dff + cols.start:dff + cols.stop]))
        for s in range(STRIPS_PER_CHUNK):
            ls = slice(s * LANES, (s + 1) * LANES)
            self.u_ext[q][s, FFN_HALO:FFN_HALO + tm, :] = a[:, ls]
            self.u_ext[q][STRIPS_PER_CHUNK + s, FFN_HALO:FFN_HALO + tm, :] = g[:, ls]

    def activate_piece(self, q, s):
        tm = self.tm
        c = q * STRIPS_PER_CHUNK + s
        for r in range(0, tm, PACKED_ROWS):
            rows = [_gelu_tanh(self._conv3(q, s, r + rr))
                    * self._conv3(q, STRIPS_PER_CHUNK + s, r + rr)
                    for rr in range(0, PACKED_ROWS, SUBLANES)]
            self.act_scr[r:r + PACKED_ROWS, c * LANES:(c + 1) * LANES] = (
                jnp.concatenate(rows, axis=0).astype(jnp.bfloat16))
        for local in (s, STRIPS_PER_CHUNK + s):
            self.u_ext[q][local, 0:FFN_HALO, :] = self.u_ext[q][local, tm:tm + FFN_HALO, :]

    def down_project(self, q):
        cols = slice(q * MXU_COLS, (q + 1) * MXU_COLS)
        packed_rows = slice(cols.start // 2, cols.stop // 2)
        self.y = self.y + _dot(self.act_scr[:, cols], _unpack(self.wdn_ref[packed_rows, :]))

    def epilogue(self):
        y = self.y
        if self.final_norm:
            y = _rmsnorm(y, self.nf_ref[...])
        self.out_tile[...] = y


class _Item:
    def __init__(self, name, mxu, emit, needs=(), valu=0, urgent=False):
        self.name, self.mxu, self.emit, self.needs, self.valu = name, mxu, emit, tuple(needs), valu
        self.urgent = urgent


def _emit_interleaved(plan, pieces):
    total_mxu = sum(it.mxu for it in plan)
    total_valu = sum(it.valu for it in plan) + sum(it.valu for it in pieces)
    done, pending = set(), list(pieces)
    spent = {"mxu": 0, "valu": 0}

    def emit(item):
        assert all(n in done for n in item.needs), (item.name, item.needs)
        item.emit()
        done.add(item.name)
        spent["mxu"] += item.mxu
        spent["valu"] += item.valu

    def emit_piece(name):
        piece = next(p for p in pending if p.name == name)
        pending.remove(piece)
        for need in piece.needs:
            if need not in done:
                emit_piece(need)
        emit(piece)

    for item in plan:
        for name in item.needs:
            if name not in done:
                emit_piece(name)
        emit(item)
        while True:
            ready = [p for p in pending if all(n in done for n in p.needs)]
            urgent = [p for p in ready if p.urgent]
            behind = spent["valu"] * total_mxu < spent["mxu"] * total_valu
            if urgent:
                emit_piece(urgent[0].name)
            elif ready and behind:
                emit_piece(ready[0].name)
            else:
                break
    for piece in list(pending):
        emit_piece(piece.name)


def _layer_kernel(*refs, tm, d, dff, tiles_per_seq, n_tiles, final_norm):
    x_ref, mixer_w, ffn_w, o_ref = refs[0], refs[1:12], refs[12:18], refs[18]
    n_mixer_scr = 2 * (d // MXU_COLS) + 8
    mixer_scr, x1_scr, ffn_scr = refs[19:19 + n_mixer_scr], refs[19 + n_mixer_scr], refs[20 + n_mixer_scr:]
    i = pl.program_id(0)
    jm = lax.rem(jnp.minimum(i, n_tiles - 1), tiles_per_seq)
    mixer = _MixerStage(x_ref.at[0], x1_scr, jm, mixer_w, mixer_scr, tm=tm, d=d)
    ffn = _FfnStage(x1_scr, o_ref.at[0], ffn_w, ffn_scr, tm=tm, d=d, dff=dff, final_norm=final_norm)

    @pl.when(i == 0)
    def _():
        x1_scr[...] = jnp.zeros((tm, d), jnp.float32)
        ffn.zero_halos()

    @pl.when(jm == 0)
    def _():
        mixer.zero_halos()

    @pl.when(lax.rem(i + tiles_per_seq - 1, tiles_per_seq) == 0)
    def _():
        ffn.zero_halos()

    ffn.begin()
    ffn.normalize_input()
    mixer.normalize_input(x_ref.at[0])

    P = functools.partial
    nq, nk = mixer.n_chunks, ffn.n_chunks
    row_groups = range(0, tm, CONV_GROUP_ROWS)
    dot_cycles = tm
    conv_names = [f"conv{q}.{s}.{r0}" for q in range(nq) for s in range(STRIPS_PER_CHUNK)
                  for r0 in row_groups]
    pool_names = [f"pool{q}.{s}" for q in range(nq) for s in range(STRIPS_PER_CHUNK)]
    norm_names = [f"norm{r0}" for r0 in row_groups]

    def conv_in(q):
        return _Item(f"PV{q}", 2 * dot_cycles, P(mixer.project_conv_in, q), valu=tm // 2)

    def pool_in(q):
        return _Item(f"PP{q}", dot_cycles, P(mixer.project_pool_in, q))

    def gates(which, ns):
        return [_Item(f"G{which}.{n}", dot_cycles, P(mixer.gate_logits, which, n)) for n in ns]

    def up(k):
        return _Item(f"UP{k}", 2 * dot_cycles, P(ffn.up_project, k))

    def down(k):
        needs = [f"act{k}.{s}" for s in range(STRIPS_PER_CHUNK)]
        return _Item(f"DN{k}", dot_cycles, P(ffn.down_project, k), needs=needs, valu=tm // 4)

    plan = [conv_in(0), pool_in(0), up(0), conv_in(1), pool_in(1), up(1)]
    mixer_fill = [gates(0, (0, 1)), [conv_in(2), pool_in(2)], gates(0, (2, 3)),
                  [conv_in(3), pool_in(3)], gates(1, (0, 1)), gates(1, (2, 3))]
    for k in range(2, nk + 2):
        if mixer_fill:
            plan += mixer_fill.pop(0)
        if k < nk:
            plan.append(up(k))
        plan.append(down(k - 2))
    assert not mixer_fill
    plan += [_Item(f"M{n}", dot_cycles * 5 // 4, P(mixer.merge_chunk, n), needs=norm_names,
                   valu=3 * tm // 4) for n in range(nq)]
    plan += [_Item(f"O{n}", dot_cycles, P(mixer.out_chunk, n), valu=tm // 8) for n in range(nq)]

    pieces = []
    for k in range(nk):
        pieces += [_Item(f"act{k}.{s}", 0, P(ffn.activate_piece, k, s), needs=[f"UP{k}"], valu=3 * tm // 4,
                         urgent=True) for s in range(STRIPS_PER_CHUNK)]
    pieces.append(_Item("ffn_out", 0, ffn.epilogue, needs=[f"DN{nk - 1}"], valu=tm))
    for q in range(nq):
        for s in range(STRIPS_PER_CHUNK):
            pieces += [_Item(f"conv{q}.{s}.{r0}", 0, P(mixer.conv_piece, q, s, r0), needs=[f"PV{q}"],
                             valu=5 * CONV_GROUP_ROWS // 2) for r0 in row_groups]
            pieces.append(_Item(f"pool{q}.{s}", 0, P(mixer.pool_piece, q, s), needs=[f"PP{q}"], valu=tm // 4))
    pieces.append(_Item("carry", 0, mixer.carry_halos, needs=conv_names + pool_names, valu=tm // 8))
    pieces += [_Item(f"norm{r0}", 0, P(mixer.normalize_piece, r0), needs=conv_names + pool_names,
                     valu=7 * CONV_GROUP_ROWS // 2) for r0 in row_groups]
    _emit_interleaved(plan, pieces)


class _Layer:
    def __init__(self, stacked, layer):
        self.stacked, self.layer = stacked, layer


def _resident(arg):
    if isinstance(arg, _Layer):
        tail = arg.stacked.shape[1:]
        index = (arg.layer,) + (0,) * len(tail)
        return pl.BlockSpec((None,) + tail, lambda i: index, pipeline_mode=pl.Buffered(1))
    zeros = (0,) * arg.ndim
    return pl.BlockSpec(arg.shape, lambda i: zeros, pipeline_mode=pl.Buffered(1))


def _operand(arg):
    return arg.stacked if isinstance(arg, _Layer) else arg


def _layer(x, mixer_args, ffn_args, *, tm, final_norm):
    bsz, seq, d = x.shape
    dff = 2 * ffn_args[4].stacked.shape[1]
    tiles_per_seq = seq // tm
    n_tiles = bsz * tiles_per_seq

    def tile_index(t):
        return (t // tiles_per_seq, t % tiles_per_seq, 0)

    x_spec = pl.BlockSpec((1, tm, d), lambda i: tile_index(jnp.minimum(i, n_tiles - 1)))
    o_spec = pl.BlockSpec((1, tm, d), lambda i: tile_index(jnp.maximum(i - 1, 0)))
    args = tuple(mixer_args) + tuple(ffn_args)
    f32_tile = pltpu.VMEM((tm, d), jnp.float32)
    bf16_tile = pltpu.VMEM((tm, d), jnp.bfloat16)
    return pl.pallas_call(
        functools.partial(_layer_kernel, tm=tm, d=d, dff=dff, tiles_per_seq=tiles_per_seq,
                          n_tiles=n_tiles, final_norm=final_norm),
        grid=(n_tiles + 1,),
        in_specs=[x_spec] + [_resident(a) for a in args],
        out_specs=o_spec,
        out_shape=jax.ShapeDtypeStruct(x.shape, x.dtype),
        scratch_shapes=(
            [pltpu.VMEM((STRIPS_PER_CHUNK, CONV_HALO + tm, LANES), jnp.float32)] * (d // MXU_COLS)
            + [pltpu.VMEM((STRIPS_PER_CHUNK, POOL_HALO + tm, LANES), jnp.float32)] * (d // MXU_COLS)
            + [bf16_tile,
               f32_tile, f32_tile, f32_tile, f32_tile,
               bf16_tile, bf16_tile, bf16_tile,
               f32_tile]
            + [pltpu.VMEM((2 * STRIPS_PER_CHUNK, FFN_HALO + tm, LANES), jnp.float32)] * (dff // MXU_COLS)
            + [bf16_tile,
               pltpu.VMEM((tm, dff), jnp.bfloat16)]),
        compiler_params=pltpu.CompilerParams(
            dimension_semantics=("arbitrary",),
            vmem_limit_bytes=VMEM_LIMIT_BYTES),
        name="layer",
    )(x, *[_operand(a) for a in args])


def kernel(x, w_in, b_in, conv_dw_w, conv_dw_b, conv_ln_g, conv_ln_b, conv_w_out, pool_w, pool_scale, mix_w_out, norm_mix, ffn_w_up, ffn_dw_w, ffn_dw_b, ffn_w_down, norm_ffn, norm_final):
    depth = w_in.shape[0]
    row = lambda v: v.reshape(1, -1)
    tm = TIME_TILE
    win, cwo, mwo = _pack_bf16(w_in), _pack_bf16(conv_w_out), _pack_bf16(mix_w_out)
    pw = _pack_bf16(pool_w.reshape(depth, -1, pool_w.shape[-1]))
    wup, wdn = _pack_bf16(ffn_w_up), _pack_bf16(ffn_w_down)
    for l in range(depth):
        mixer_args = (row(norm_mix[l]), _Layer(win, l), row(b_in[l]), conv_dw_w[l],
                      row(conv_dw_b[l]), row(conv_ln_g[l]), row(conv_ln_b[l]), _Layer(cwo, l),
                      _Layer(pw, l), row(pool_scale[l]), _Layer(mwo, l))
        ffn_args = (row(norm_ffn[l]), _Layer(wup, l), ffn_dw_w[l], row(ffn_dw_b[l]),
                    _Layer(wdn, l), row(norm_final))
        x = _layer(x, mixer_args, ffn_args, tm=tm, final_norm=(l == depth - 1))
    return x
```

```python
import functools
import math

import jax
import jax.numpy as jnp
from jax import lax
from jax.experimental import pallas as pl
from jax.experimental.pallas import tpu as pltpu

LANES = 128
SUBLANES = 8
MXU_COLS = 256
VMEM_LIMIT_BYTES = 60 * 1024 * 1024

EPS = 1e-6
CONV_KERNEL = 31
POOL_WINDOWS = (2, 4, 8, 16)
FFN_KERNEL = 3

CONV_HALO = 32
POOL_HALO = 16
FFN_HALO = 8
PACKED_ROWS = 16
CONV_GROUP_ROWS = 32
PACK_BLOCK_BYTES = 6 * 1024 * 1024
TIME_TILE = 256
STRIPS_PER_CHUNK = MXU_COLS // LANES

GELU_C = math.sqrt(2.0 / math.pi)


def _rmsnorm(x, g):
    ms = jnp.mean(x * x, axis=-1, keepdims=True)
    return x * lax.rsqrt(ms + EPS) * g


def _sigmoid(x):
    return 0.5 + 0.5 * jnp.tanh(0.5 * x)


def _gelu_tanh(x):
    hx = 0.5 * x
    return hx + hx * jnp.tanh(x * (GELU_C + (0.044715 * GELU_C) * (x * x)))


def _dot(a, b):
    return jnp.dot(a, b, preferred_element_type=jnp.float32)


def _pack_kernel(w_ref, o_ref):
    o_ref[...] = pltpu.bitcast(w_ref[...].astype(jnp.bfloat16), jnp.int32)


def _pack_bf16(w):
    n_layers, k, n = w.shape
    bk = max(b for b in range(PACKED_ROWS, k + 1, PACKED_ROWS)
             if k % b == 0 and b * n * 4 <= PACK_BLOCK_BYTES)
    return pl.pallas_call(
        _pack_kernel,
        grid=(n_layers, k // bk),
        in_specs=[pl.BlockSpec((None, bk, n), lambda l, i: (l, i, 0))],
        out_specs=pl.BlockSpec((None, bk // 2, n), lambda l, i: (l, i, 0)),
        out_shape=jax.ShapeDtypeStruct((n_layers, k // 2, n), jnp.int32),
        compiler_params=pltpu.CompilerParams(
            dimension_semantics=("arbitrary", "arbitrary"),
            vmem_limit_bytes=VMEM_LIMIT_BYTES),
        name="pack_bf16",
    )(w)


def _unpack(packed):
    return pltpu.bitcast(packed, jnp.bfloat16)


def _vreg_row(ref, k, cs):
    return jnp.broadcast_to(ref[k:k + 1, cs], (SUBLANES, LANES))


class _MixerStage:
    def __init__(self, x_tile, out_tile, j, weights, scratch, *, tm, d):
        (self.nrm_ref, self.win_ref, self.bin_ref, self.dww_ref, self.dwb_ref, self.lng_ref,
         self.lnb_ref, self.cwo_ref, self.pw_ref, self.ps_ref, self.mwo_ref) = weights
        self.n_chunks = d // MXU_COLS
        assert self.n_chunks == len(POOL_WINDOWS)
        self.u_ext, scratch = scratch[:self.n_chunks], scratch[self.n_chunks:]
        self.p_ext, scratch = scratch[:self.n_chunks], scratch[self.n_chunks:]
        (self.h_scr, self.conv_scr, self.pooled_scr, self.ga_scr,
         self.gb_scr, self.act_scr, self.pooled_bf, self.merged_scr) = scratch
        self.x_tile, self.out_tile, self.j, self.tm, self.d = x_tile, out_tile, j, tm, d

    def zero_halos(self):
        for q in range(self.n_chunks):
            self.u_ext[q][:, 0:CONV_HALO, :] = jnp.zeros((STRIPS_PER_CHUNK, CONV_HALO, LANES), jnp.float32)
            self.p_ext[q][:, 0:POOL_HALO, :] = jnp.zeros((STRIPS_PER_CHUNK, POOL_HALO, LANES), jnp.float32)

    def normalize_input(self, x_tile):
        self.h_scr[...] = _rmsnorm(x_tile[...], self.nrm_ref[...]).astype(jnp.bfloat16)

    def _proj(self, col0, cols):
        sl = slice(col0 + cols.start, col0 + cols.stop)
        return _dot(self.h_scr[...], _unpack(self.win_ref[:, sl])) + self.bin_ref[:, sl]

    def project_conv_in(self, q):
        tm, d = self.tm, self.d
        cols = slice(q * MXU_COLS, (q + 1) * MXU_COLS)
        u = self._proj(0, cols) * _sigmoid(self._proj(d, cols))
        for s in range(STRIPS_PER_CHUNK):
            self.u_ext[q][s, CONV_HALO:CONV_HALO + tm, :] = u[:, s * LANES:(s + 1) * LANES]

    def project_pool_in(self, q):
        tm, d = self.tm, self.d
        pin = self._proj(2 * d, slice(q * MXU_COLS, (q + 1) * MXU_COLS))
        for s in range(STRIPS_PER_CHUNK):
            self.p_ext[q][s, POOL_HALO:POOL_HALO + tm, :] = pin[:, s * LANES:(s + 1) * LANES]

    def conv_piece(self, q, s, r0):
        conv_base = CONV_HALO - (CONV_KERNEL - 1)
        c = q * STRIPS_PER_CHUNK + s
        cs = slice(c * LANES, (c + 1) * LANES)
        ws = [_vreg_row(self.dww_ref, k, cs) for k in range(CONV_KERNEL)]
        blocks = range(r0, r0 + CONV_GROUP_ROWS, SUBLANES)
        accs = {r: _vreg_row(self.dwb_ref, 0, cs) for r in blocks}
        for lo in range(r0 + conv_base, r0 + conv_base + CONV_GROUP_ROWS + CONV_KERNEL - 1):
            taps = [(r, lo - conv_base - r) for r in blocks
                    if 0 <= lo - conv_base - r < CONV_KERNEL]
            win = self.u_ext[q][s, lo:lo + SUBLANES, :]
            for r, k in taps:
                accs[r] = accs[r] + ws[k] * win
        for r in blocks:
            self.conv_scr[r:r + SUBLANES, cs] = accs[r]

    def pool_piece(self, q, s):
        tm = self.tm
        row_iota = lax.broadcasted_iota(jnp.int32, (SUBLANES, LANES), 0)
        window = POOL_WINDOWS[q]
        c = q * STRIPS_PER_CHUNK + s
        cs = slice(c * LANES, (c + 1) * LANES)
        for r in range(0, tm, SUBLANES):
            tok = self.p_ext[q][s, r + POOL_HALO:r + POOL_HALO + SUBLANES, :]
            acc = tok
            for back in range(1, window):
                lo = r + POOL_HALO - back
                acc = acc + self.p_ext[q][s, lo:lo + SUBLANES, :]
            pos1 = (self.j * tm + (r + 1) + row_iota).astype(jnp.float32)
            cnt = jnp.minimum(pos1, float(window))
            self.pooled_scr[r:r + SUBLANES, cs] = acc / cnt - tok

    def gate_logits(self, which, n):
        col0, dst = ((3 * self.d, self.ga_scr), (4 * self.d, self.gb_scr))[which]
        cols = slice(n * MXU_COLS, (n + 1) * MXU_COLS)
        dst[:, cols] = _dot(self.h_scr[...],
                            _unpack(self.win_ref[:, col0 + cols.start:col0 + cols.stop]))

    def _gate(self, which, cols):
        col0, src = ((3 * self.d, self.ga_scr), (4 * self.d, self.gb_scr))[which]
        return _sigmoid(src[:, cols] + self.bin_ref[:, col0 + cols.start:col0 + cols.stop])

    def carry_halos(self):
        tm = self.tm
        for q in range(self.n_chunks):
            self.u_ext[q][:, 0:CONV_HALO, :] = self.u_ext[q][:, tm:tm + CONV_HALO, :]
            self.p_ext[q][:, 0:POOL_HALO, :] = self.p_ext[q][:, tm:tm + POOL_HALO, :]

    def normalize_piece(self, r0):
        rows = slice(r0, r0 + CONV_GROUP_ROWS)
        y = self.conv_scr[rows, :]
        mu = jnp.mean(y, axis=-1, keepdims=True)
        yc = y - mu
        var = jnp.mean(yc * yc, axis=-1, keepdims=True)
        yn = yc * lax.rsqrt(var + EPS) * self.lng_ref[...] + self.lnb_ref[...]
        self.act_scr[rows, :] = (yn * _sigmoid(yn)).astype(jnp.bfloat16)
        self.pooled_bf[rows, :] = self.pooled_scr[rows, :].astype(jnp.bfloat16)

    def merge_chunk(self, n):
        cols = slice(n * MXU_COLS, (n + 1) * MXU_COLS)
        ya = _dot(self.act_scr[...], _unpack(self.cwo_ref[:, cols]))
        group_rows = slice(cols.start // 2, cols.stop // 2)
        yb = _dot(self.pooled_bf[:, cols], _unpack(self.pw_ref[group_rows, :])) * self.ps_ref[:, cols]
        merged = self._gate(0, cols) * ya + self._gate(1, cols) * yb
        self.merged_scr[:, cols] = merged.astype(jnp.bfloat16)

    def out_chunk(self, n):
        cols = slice(n * MXU_COLS, (n + 1) * MXU_COLS)
        self.out_tile[:, cols] = self.x_tile[:, cols] + _dot(
            self.merged_scr[...], _unpack(self.mwo_ref[:, cols]))


class _FfnStage:
    def __init__(self, x_tile, out_tile, weights, scratch, *, tm, d, dff, final_norm):
        self.nrm_ref, self.wup_ref, self.dww_ref, self.dwb_ref, self.wdn_ref, self.nf_ref = weights
        self.n_chunks = dff // MXU_COLS
        self.u_ext, (self.h_scr, self.act_scr) = scratch[:self.n_chunks], scratch[self.n_chunks:]
        self.x_tile, self.out_tile = x_tile, out_tile
        self.tm, self.d, self.dff, self.final_norm = tm, d, dff, final_norm
        self.y = None

    def zero_halos(self):
        for q in range(self.n_chunks):
            self.u_ext[q][:, 0:FFN_HALO, :] = jnp.zeros((2 * STRIPS_PER_CHUNK, FFN_HALO, LANES), jnp.float32)

    def normalize_input(self):
        self.h_scr[...] = _rmsnorm(self.x_tile[...], self.nrm_ref[...]).astype(jnp.bfloat16)

    def begin(self):
        self.out_tile[...] = self.x_tile[...]

    def _conv3(self, q, local, r):
        base = FFN_HALO - (FFN_KERNEL - 1)
        is_gate, s = divmod(local, STRIPS_PER_CHUNK)
        col0 = is_gate * self.dff + (q * STRIPS_PER_CHUNK + s) * LANES
        cs = slice(col0, col0 + LANES)
        strip = self.u_ext[q]
        acc = (_vreg_row(self.dww_ref, 0, cs) * strip[local, r + base:r + base + SUBLANES, :]
               + _vreg_row(self.dwb_ref, 0, cs))
        for k in range(1, FFN_KERNEL):
            lo = r + base + k
            acc = acc + _vreg_row(self.dww_ref, k, cs) * strip[local, lo:lo + SUBLANES, :]
        return acc

    def up_project(self, q):
        tm, dff = self.tm, self.dff
        cols = slice(q * MXU_COLS, (q + 1) * MXU_COLS)
        a = _dot(self.h_scr[...], _unpack(self.wup_ref[:, cols]))
        g = _dot(self.h_scr[...], _unpack(self.wup_ref[:, dff + cols.start:dff + cols.stop]))
        for s in range(STRIPS_PER_CHUNK):
            ls = slice(s * LANES, (s + 1) * LANES)
            self.u_ext[q][s, FFN_HALO:FFN_HALO + tm, :] = a[:, ls]
            self.u_ext[q][STRIPS_PER_CHUNK + s, FFN_HALO:FFN_HALO + tm, :] = g[:, ls]

    def activate_piece(self, q, s):
        tm = self.tm
        c = q * STRIPS_PER_CHUNK + s
        for r in range(0, tm, PACKED_ROWS):
            rows = [_gelu_tanh(self._conv3(q, s, r + rr))
                    * self._conv3(q, STRIPS_PER_CHUNK + s, r + rr)
                    for rr in range(0, PACKED_ROWS, SUBLANES)]
            self.act_scr[r:r + PACKED_ROWS, c * LANES:(c + 1) * LANES] = (
                jnp.concatenate(rows, axis=0).astype(jnp.bfloat16))
        for local in (s, STRIPS_PER_CHUNK + s):
            self.u_ext[q][local, 0:FFN_HALO, :] = self.u_ext[q][local, tm:tm + FFN_HALO, :]

    def down_project(self, q):
        cols = slice(q * MXU_COLS, (q + 1) * MXU_COLS)
        packed_rows = slice(cols.start // 2, cols.stop // 2)
        self.out_tile[...] += _dot(self.act_scr[:, cols], _unpack(self.wdn_ref[packed_rows, :]))

    def epilogue(self):
        if self.final_norm:
            self.out_tile[...] = _rmsnorm(self.out_tile[...], self.nf_ref[...])


class _Item:
    def __init__(self, name, mxu, emit, needs=(), valu=0, urgent=False):
        self.name, self.mxu, self.emit, self.needs, self.valu = name, mxu, emit, tuple(needs), valu
        self.urgent = urgent


def _emit_interleaved(plan, pieces):
    total_mxu = sum(it.mxu for it in plan)
    total_valu = sum(it.valu for it in plan) + sum(it.valu for it in pieces)
    done, pending = set(), list(pieces)
    spent = {"mxu": 0, "valu": 0}

    def emit(item):
        assert all(n in done for n in item.needs), (item.name, item.needs)
        item.emit()
        done.add(item.name)
        spent["mxu"] += item.mxu
        spent["valu"] += item.valu

    def emit_piece(name):
        piece = next(p for p in pending if p.name == name)
        pending.remove(piece)
        for need in piece.needs:
            if need not in done:
                emit_piece(need)
        emit(piece)

    for item in plan:
        for name in item.needs:
            if name not in done:
                emit_piece(name)
        emit(item)
        while True:
            ready = [p for p in pending if all(n in done for n in p.needs)]
            urgent = [p for p in ready if p.urgent]
            behind = spent["valu"] * total_mxu < spent["mxu"] * total_valu
            if urgent:
                emit_piece(urgent[0].name)
            elif ready and behind:
                emit_piece(ready[0].name)
            else:
                break
    for piece in list(pending):
        emit_piece(piece.name)


def _layer_kernel(*refs, tm, d, dff, tiles_per_seq, n_tiles, final_norm):
    x_ref, mixer_w, ffn_w, o_ref = refs[0], refs[1:12], refs[12:18], refs[18]
    n_mixer_scr = 2 * (d // MXU_COLS) + 8
    mixer_scr, x1_scr, ffn_scr = refs[19:19 + n_mixer_scr], refs[19 + n_mixer_scr], refs[20 + n_mixer_scr:]
    i = pl.program_id(0)
    jm = lax.rem(jnp.minimum(i, n_tiles - 1), tiles_per_seq)
    mixer = _MixerStage(x_ref.at[0], x1_scr, jm, mixer_w, mixer_scr, tm=tm, d=d)
    ffn = _FfnStage(x1_scr, o_ref.at[0], ffn_w, ffn_scr, tm=tm, d=d, dff=dff, final_norm=final_norm)

    @pl.when(i == 0)
    def _():
        x1_scr[...] = jnp.zeros((tm, d), jnp.float32)
        ffn.zero_halos()

    @pl.when(jm == 0)
    def _():
        mixer.zero_halos()

    @pl.when(lax.rem(i + tiles_per_seq - 1, tiles_per_seq) == 0)
    def _():
        ffn.zero_halos()

    ffn.begin()
    ffn.normalize_input()
    mixer.normalize_input(x_ref.at[0])

    P = functools.partial
    nq, nk = mixer.n_chunks, ffn.n_chunks
    row_groups = range(0, tm, CONV_GROUP_ROWS)
    dot_cycles = tm
    conv_names = [f"conv{q}.{s}.{r0}" for q in range(nq) for s in range(STRIPS_PER_CHUNK)
                  for r0 in row_groups]
    pool_names = [f"pool{q}.{s}" for q in range(nq) for s in range(STRIPS_PER_CHUNK)]
    norm_names = [f"norm{r0}" for r0 in row_groups]

    def conv_in(q):
        return _Item(f"PV{q}", 2 * dot_cycles, P(mixer.project_conv_in, q), valu=tm // 2)

    def pool_in(q):
        return _Item(f"PP{q}", dot_cycles, P(mixer.project_pool_in, q))

    def gates(which, ns):
        return [_Item(f"G{which}.{n}", dot_cycles, P(mixer.gate_logits, which, n)) for n in ns]

    def up(k):
        return _Item(f"UP{k}", 2 * dot_cycles, P(ffn.up_project, k))

    def down(k):
        needs = [f"act{k}.{s}" for s in range(STRIPS_PER_CHUNK)]
        return _Item(f"DN{k}", dot_cycles, P(ffn.down_project, k), needs=needs, valu=tm // 4)

    plan = [conv_in(0), pool_in(0), up(0), conv_in(1), pool_in(1), up(1)]
    mixer_fill = [gates(0, (0, 1)), [conv_in(2), pool_in(2)], gates(0, (2, 3)),
                  [conv_in(3), pool_in(3)], gates(1, (0, 1)), gates(1, (2, 3))]
    for k in range(2, nk + 2):
        if mixer_fill:
            plan += mixer_fill.pop(0)
        if k < nk:
            plan.append(up(k))
        plan.append(down(k - 2))
    assert not mixer_fill
    plan += [_Item(f"M{n}", dot_cycles * 5 // 4, P(mixer.merge_chunk, n), needs=norm_names,
                   valu=3 * tm // 4) for n in range(nq)]
    plan += [_Item(f"O{n}", dot_cycles, P(mixer.out_chunk, n), valu=tm // 8) for n in range(nq)]

    pieces = []
    for k in range(nk):
        pieces += [_Item(f"act{k}.{s}", 0, P(ffn.activate_piece, k, s), needs=[f"UP{k}"], valu=3 * tm // 4,
                         urgent=True) for s in range(STRIPS_PER_CHUNK)]
    pieces.append(_Item("ffn_out", 0, ffn.epilogue, needs=[f"DN{nk - 1}"], valu=tm))
    for q in range(nq):
        for s in range(STRIPS_PER_CHUNK):
            pieces += [_Item(f"conv{q}.{s}.{r0}", 0, P(mixer.conv_piece, q, s, r0), needs=[f"PV{q}"],
                             valu=5 * CONV_GROUP_ROWS // 2) for r0 in row_groups]
            pieces.append(_Item(f"pool{q}.{s}", 0, P(mixer.pool_piece, q, s), needs=[f"PP{q}"], valu=tm // 4))
    pieces.append(_Item("carry", 0, mixer.carry_halos, needs=conv_names + pool_names, valu=tm // 8))
    pieces += [_Item(f"norm{r0}", 0, P(mixer.normalize_piece, r0), needs=conv_names + pool_names,
                     valu=7 * CONV_GROUP_ROWS // 2) for r0 in row_groups]
    _emit_interleaved(plan, pieces)


class _Layer:
    def __init__(self, stacked, layer):
        self.stacked, self.layer = stacked, layer


def _resident(arg):
    if isinstance(arg, _Layer):
        tail = arg.stacked.shape[1:]
        index = (arg.layer,) + (0,) * len(tail)
        return pl.BlockSpec((None,) + tail, lambda i: index, pipeline_mode=pl.Buffered(1))
    zeros = (0,) * arg.ndim
    return pl.BlockSpec(arg.shape, lambda i: zeros, pipeline_mode=pl.Buffered(1))


def _operand(arg):
    return arg.stacked if isinstance(arg, _Layer) else arg


def _layer(x, mixer_args, ffn_args, *, tm, final_norm):
    bsz, seq, d = x.shape
    dff = 2 * ffn_args[4].stacked.shape[1]
    tiles_per_seq = seq // tm
    n_tiles = bsz * tiles_per_seq

    def tile_index(t):
        return (t // tiles_per_seq, t % tiles_per_seq, 0)

    x_spec = pl.BlockSpec((1, tm, d), lambda i: tile_index(jnp.minimum(i, n_tiles - 1)))
    o_spec = pl.BlockSpec((1, tm, d), lambda i: tile_index(jnp.maximum(i - 1, 0)))
    args = tuple(mixer_args) + tuple(ffn_args)
    f32_tile = pltpu.VMEM((tm, d), jnp.float32)
    bf16_tile = pltpu.VMEM((tm, d), jnp.bfloat16)
    return pl.pallas_call(
        functools.partial(_layer_kernel, tm=tm, d=d, dff=dff, tiles_per_seq=tiles_per_seq,
                          n_tiles=n_tiles, final_norm=final_norm),
        grid=(n_tiles + 1,),
        in_specs=[x_spec] + [_resident(a) for a in args],
        out_specs=o_spec,
        out_shape=jax.ShapeDtypeStruct(x.shape, x.dtype),
        scratch_shapes=(
            [pltpu.VMEM((STRIPS_PER_CHUNK, CONV_HALO + tm, LANES), jnp.float32)] * (d // MXU_COLS)
            + [pltpu.VMEM((STRIPS_PER_CHUNK, POOL_HALO + tm, LANES), jnp.float32)] * (d // MXU_COLS)
            + [bf16_tile,
               f32_tile, f32_tile, f32_tile, f32_tile,
               bf16_tile, bf16_tile, bf16_tile,
               f32_tile]
            + [pltpu.VMEM((2 * STRIPS_PER_CHUNK, FFN_HALO + tm, LANES), jnp.float32)] * (dff // MXU_COLS)
            + [bf16_tile,
               pltpu.VMEM((tm, dff), jnp.bfloat16)]),
        compiler_params=pltpu.CompilerParams(
            dimension_semantics=("arbitrary",),
            vmem_limit_bytes=VMEM_LIMIT_BYTES),
        name="layer",
    )(x, *[_operand(a) for a in args])


def kernel(x, w_in, b_in, conv_dw_w, conv_dw_b, conv_ln_g, conv_ln_b, conv_w_out, pool_w, pool_scale, mix_w_out, norm_mix, ffn_w_up, ffn_dw_w, ffn_dw_b, ffn_w_down, norm_ffn, norm_final):
    depth = w_in.shape[0]
    row = lambda v: v.reshape(1, -1)
    tm = TIME_TILE
    win, cwo, mwo = _pack_bf16(w_in), _pack_bf16(conv_w_out), _pack_bf16(mix_w_out)
    pw = _pack_bf16(pool_w.reshape(depth, -1, pool_w.shape[-1]))
    wup, wdn = _pack_bf16(ffn_w_up), _pack_bf16(ffn_w_down)
    for l in range(depth):
        mixer_args = (row(norm_mix[l]), _Layer(win, l), row(b_in[l]), conv_dw_w[l],
                      row(conv_dw_b[l]), row(conv_ln_g[l]), row(conv_ln_b[l]), _Layer(cwo, l),
                      _Layer(pw, l), row(pool_scale[l]), _Layer(mwo, l))
        ffn_args = (row(norm_ffn[l]), _Layer(wup, l), ffn_dw_w[l], row(ffn_dw_b[l]),
                    _Layer(wdn, l), row(norm_final))
        x = _layer(x, mixer_args, ffn_args, tm=tm, final_norm=(l == depth - 1))
    return x
```
